```python
import math
import jax, jax.numpy as jnp
from jax import lax
import numpy as np

D_MODEL = 1024
BATCH = 8
SEQ = 2048
DEPTH = 4
DEC_BATCH = 128
DEC_SEQ = 8
PAST_LEN = 2048
PAGE_SIZE = 128

REL_HEADS = 8
REL_BUCKETS = 32
REL_MAX_DIST = 128
A_HEADS = REL_HEADS
A_WIDTH = D_MODEL // 2
A_DIM = A_WIDTH // A_HEADS
IDX_HEADS = 8
IDX_DIM = 64
DSA_TOPK = 256
B_HEADS = 8
B_WIDTH = D_MODEL // 2
B_DIM = B_WIDTH // B_HEADS
C_HEADS = REL_HEADS
C_WIDTH = D_MODEL
C_DIM = C_WIDTH // C_HEADS
MOBA_BLOCK = 256
MOBA_TOPK = 3
QUERY_BLOCK = 128
ROW_CHUNK = 32
EPS = 1e-6
AB_SIZES = (A_WIDTH, A_WIDTH, A_WIDTH, A_WIDTH, IDX_HEADS * IDX_DIM, IDX_DIM, IDX_HEADS, B_WIDTH, B_WIDTH, B_WIDTH, B_WIDTH)
AB_IN = sum(AB_SIZES)
C_IN = 4 * C_WIDTH
N_AB = (DEPTH + 1) // 2
N_C = DEPTH // 2

kernel_name = 'hybrid_dsa_stickbreak_moba_decode_step'


def _rms(x):
    x32 = x.astype(jnp.float32)
    return x32 * lax.rsqrt(jnp.mean(x32 * x32, axis=-1, keepdims=True) + EPS)


def _rmsnorm(x, w):
    return (_rms(x) * w.astype(jnp.float32)).astype(x.dtype)


def _split(x, sizes):
    offs = np.cumsum(np.array(sizes))[:-1].tolist()
    return jnp.split(x, offs, axis=-1)


def _t5_bucket(dist):
    dist = jnp.maximum(dist, 0)
    max_exact = REL_BUCKETS // 2
    d = jnp.maximum(dist, max_exact).astype(jnp.float32)
    large = max_exact + (jnp.log(d / max_exact) / math.log(REL_MAX_DIST / max_exact) * (REL_BUCKETS - max_exact)).astype(jnp.int32)
    large = jnp.minimum(large, REL_BUCKETS - 1)
    return jnp.where(dist < max_exact, dist, large)


def _query_rows(B, T, qpos):
    b_id = jnp.repeat(jnp.arange(B, dtype=jnp.int32), T)
    pos = jnp.tile(qpos, B)
    return b_id, pos


def _row_map(fn, rows, n_rows):
    r = math.gcd(n_rows, ROW_CHUNK)
    chunked = tuple(a.reshape((n_rows // r, r) + a.shape[1:]) for a in rows)
    out = lax.map(fn, chunked)
    return out.reshape((n_rows,) + out.shape[2:])


def _dsa_attention(q, qi, wi, k, v, kidx, qpos, rel_bias):
    B, T, H, D = q.shape
    L = k.shape[1]
    top = min(DSA_TOPK, L // 4)
    key_pos = jnp.arange(L, dtype=jnp.int32)
    scale = D ** -0.5
    idx_scale = IDX_DIM ** -0.5

    def chunk(args):
        qr, qir, wir, br, pr = args
        s = jnp.einsum('rjd,rld->rjl', qir, kidx[br]).astype(jnp.float32) * idx_scale
        score = jnp.einsum('rj,rjl->rl', wir.astype(jnp.float32), jax.nn.relu(s))
        score = jnp.where(key_pos[None, :] <= pr[:, None], score, -jnp.inf)
        _, sel = lax.top_k(score, top)
        valid = sel <= pr[:, None]
        kg = k[br[:, None], sel]
        vg = v[br[:, None], sel]
        bias = rel_bias[_t5_bucket(pr[:, None] - sel)].astype(jnp.float32)
        logits = jnp.einsum('rhd,rkhd->rhk', qr, kg).astype(jnp.float32) * scale + jnp.swapaxes(bias, 1, 2)
        logits = jnp.where(valid[:, None, :], logits, -jnp.inf)
        p = jax.nn.softmax(logits, axis=-1)
        return jnp.einsum('rhk,rkhd->rhd', p.astype(vg.dtype), vg)

    b_id, pos = _query_rows(B, T, qpos)
    n = B * T
    out = _row_map(chunk, (q.reshape(n, H, D), qi.reshape(n, IDX_HEADS, IDX_DIM), wi.reshape(n, IDX_HEADS), b_id, pos), n)
    return out.reshape(B, T, H, D)


def _stick_breaking(q, k, v, qpos):
    B, T, H, D = q.shape
    L = k.shape[1]
    qb_size = QUERY_BLOCK if T % QUERY_BLOCK == 0 else T
    nq = T // qb_size
    qb = q.reshape(B, nq, qb_size, H, D).transpose(1, 0, 2, 3, 4)
    pb = qpos.reshape(nq, qb_size)
    key_pos = jnp.arange(L, dtype=jnp.int32)
    scale = D ** -0.5

    def block(args):
        qq, pp = args
        z = jnp.einsum('bqhd,blhd->bhql', qq, k).astype(jnp.float32) * scale
        mask = key_pos[None, :] < pp[:, None]
        log_beta = jax.nn.log_sigmoid(z)
        log_1m = jnp.where(mask, jax.nn.log_sigmoid(-z), 0.0)
        suffix = lax.cumsum(log_1m, axis=3, reverse=True) - log_1m
        w = jnp.where(mask, jnp.exp(log_beta + suffix), 0.0)
        return jnp.einsum('bhql,blhd->bqhd', w.astype(v.dtype), v)

    out = lax.map(block, (qb, pb))
    return out.transpose(1, 0, 2, 3, 4).reshape(B, T, H, D)


def _moba_attention(q, k, v, qpos, rel_bias):
    B, T, H, D = q.shape
    L = k.shape[1]
    nb = -(-L // MOBA_BLOCK)
    pad = nb * MOBA_BLOCK - L

    def to_blocks(a):
        a = jnp.pad(a, ((0, 0), (0, pad), (0, 0), (0, 0)))
        return a.reshape(B, nb, MOBA_BLOCK, H, D).transpose(0, 3, 1, 2, 4)

    kb, vb = to_blocks(k), to_blocks(v)
    kmean = jnp.mean(kb.astype(jnp.float32), axis=3)
    nsel = min(MOBA_TOPK, nb)
    h_ids = jnp.arange(H, dtype=jnp.int32)
    offs = jnp.arange(MOBA_BLOCK, dtype=jnp.int32)
    blk_ids = jnp.arange(nb, dtype=jnp.int32)
    scale = D ** -0.5

    def chunk(args):
        qr, br, pr = args
        r = qr.shape[0]
        own = pr // MOBA_BLOCK
        gate = jnp.einsum('rhd,rhnd->rhn', qr.astype(jnp.float32), kmean[br])
        gate = jnp.where(blk_ids[None, None, :] < own[:, None, None], gate, -jnp.inf)
        _, sel = lax.top_k(gate, nsel)
        sel_ok = sel < own[:, None, None]
        own_b = jnp.broadcast_to(own[:, None, None], (r, H, 1)).astype(sel.dtype)
        chosen = jnp.concatenate([sel, own_b], axis=-1)
        ok = jnp.concatenate([sel_ok, jnp.ones((r, H, 1), dtype=bool)], axis=-1)
        kg = kb[br[:, None, None], h_ids[None, :, None], chosen]
        vg = vb[br[:, None, None], h_ids[None, :, None], chosen]
        kpos = chosen[..., None] * MOBA_BLOCK + offs
        valid = ok[..., None] & (kpos <= pr[:, None, None, None])
        bias = rel_bias[_t5_bucket(pr[:, None, None, None] - kpos), h_ids[None, :, None, None]].astype(jnp.float32)
        logits = jnp.einsum('rhd,rhnkd->rhnk', qr, kg).astype(jnp.float32) * scale + bias
        logits = jnp.where(valid, logits, -jnp.inf)
        p = jax.nn.softmax(logits.reshape(r, H, -1), axis=-1).reshape(logits.shape)
        return jnp.einsum('rhnk,rhnkd->rhd', p.astype(vg.dtype), vg)

    b_id, pos = _query_rows(B, T, qpos)
    n = B * T
    out = _row_map(chunk, (q.reshape(n, H, D), b_id, pos), n)
    return out.reshape(B, T, H, D)


def _ab_mixer(h, past, p0, w_in, w_out, qn, kn, rel_bias):
    B, T, _ = h.shape
    qa, ka, va, ga, qi, ki, wi, qb, kb, vb, gb = _split(h @ w_in, AB_SIZES)
    qa = _rmsnorm(qa.reshape(B, T, A_HEADS, A_DIM), qn)
    ka = _rmsnorm(ka.reshape(B, T, A_HEADS, A_DIM), kn)
    va = va.reshape(B, T, A_HEADS, A_DIM)
    qi = qi.reshape(B, T, IDX_HEADS, IDX_DIM)
    ki = _rms(ki).astype(h.dtype)
    wi = wi * (IDX_HEADS ** -0.5)
    qb = qb.reshape(B, T, B_HEADS, B_DIM)
    kb = kb.reshape(B, T, B_HEADS, B_DIM)
    vb = vb.reshape(B, T, B_HEADS, B_DIM)
    a_kv_new = jnp.stack([ka, va], axis=2)
    b_kv_new = jnp.stack([kb, vb], axis=2)
    a_kv, a_kidx, b_kv = a_kv_new, ki, b_kv_new
    if past is not None:
        a_kv = jnp.concatenate([past[0].astype(a_kv_new.dtype), a_kv_new], axis=1)
        a_kidx = jnp.concatenate([past[1].astype(ki.dtype), ki], axis=1)
        b_kv = jnp.concatenate([past[2].astype(b_kv_new.dtype), b_kv_new], axis=1)
    qpos = p0 + jnp.arange(T, dtype=jnp.int32)
    oa = _dsa_attention(qa, qi, wi, a_kv[:, :, 0], a_kv[:, :, 1], a_kidx, qpos, rel_bias)
    ob = _stick_breaking(qb, b_kv[:, :, 0], b_kv[:, :, 1], qpos)
    mixed = jnp.concatenate([oa.reshape(B, T, A_WIDTH) * jax.nn.silu(ga), ob.reshape(B, T, B_WIDTH) * jax.nn.silu(gb)], axis=-1)
    return mixed @ w_out, (a_kv_new, ki, b_kv_new)


def _c_mixer(h, past, p0, w_in, w_out, qn, kn, rel_bias):
    B, T, _ = h.shape
    q, k, v, g = _split(h @ w_in, (C_WIDTH, C_WIDTH, C_WIDTH, C_WIDTH))
    q = _rmsnorm(q.reshape(B, T, C_HEADS, C_DIM), qn)
    k = _rmsnorm(k.reshape(B, T, C_HEADS, C_DIM), kn)
    v = v.reshape(B, T, C_HEADS, C_DIM)
    c_kv_new = jnp.stack([k, v], axis=2)
    c_kv = c_kv_new
    if past is not None:
        c_kv = jnp.concatenate([past.astype(c_kv_new.dtype), c_kv_new], axis=1)
    qpos = p0 + jnp.arange(T, dtype=jnp.int32)
    o = _moba_attention(q, c_kv[:, :, 0], c_kv[:, :, 1], qpos, rel_bias)
    return (o.reshape(B, T, C_WIDTH) * jax.nn.silu(g)) @ w_out, c_kv_new


def setup_inputs(seed: int = 0) -> dict:
    key = jax.random.key(seed)
    ks = jax.random.split(key, 20)
    n_pages = PAST_LEN // PAGE_SIZE
    n_pool = (DEC_BATCH * n_pages * 5) // 4
    f32 = jnp.float32

    def nrm(k, shape, scale=1.0):
        return jax.random.normal(k, shape, f32) * scale

    page_table = jax.random.permutation(ks[6], n_pool)[:DEC_BATCH * n_pages].reshape(DEC_BATCH, n_pages).astype(jnp.int32)
    return {
        'x_prompt': nrm(ks[0], (BATCH, SEQ, D_MODEL)),
        'x_sample': nrm(ks[1], (DEC_BATCH, DEC_SEQ, D_MODEL)),
        'cache_a_kv': nrm(ks[2], (N_AB, n_pool, PAGE_SIZE, 2, A_HEADS, A_DIM)),
        'cache_a_kidx': nrm(ks[3], (N_AB, n_pool, PAGE_SIZE, IDX_DIM)),
        'cache_b_kv': nrm(ks[4], (N_AB, n_pool, PAGE_SIZE, 2, B_HEADS, B_DIM)),
        'cache_c_kv': nrm(ks[5], (N_C, n_pool, PAGE_SIZE, 2, C_HEADS, C_DIM)),
        'page_table': page_table,
        'ln_w': 1.0 + nrm(ks[7], (DEPTH, D_MODEL), 0.02),
        'w_in_ab': nrm(ks[8], (N_AB, D_MODEL, AB_IN), D_MODEL ** -0.5),
        'w_out_ab': nrm(ks[9], (N_AB, A_WIDTH + B_WIDTH, D_MODEL), (A_WIDTH + B_WIDTH) ** -0.5),
        'qn_a': 1.0 + nrm(ks[10], (N_AB, A_DIM), 0.02),
        'kn_a': 1.0 + nrm(ks[11], (N_AB, A_DIM), 0.02),
        'w_in_c': nrm(ks[12], (N_C, D_MODEL, C_IN), D_MODEL ** -0.5),
        'w_out_c': nrm(ks[13], (N_C, C_WIDTH, D_MODEL), C_WIDTH ** -0.5),
        'qn_c': 1.0 + nrm(ks[14], (N_C, C_DIM), 0.02),
        'kn_c': 1.0 + nrm(ks[15], (N_C, C_DIM), 0.02),
        'rel_bias': nrm(ks[16], (REL_BUCKETS, REL_HEADS), 0.5),
    }


def reference(x_prompt, x_sample, cache_a_kv, cache_a_kidx, cache_b_kv, cache_c_kv, page_table, ln_w, w_in_ab, w_out_ab, qn_a, kn_a, w_in_c, w_out_c, qn_c, kn_c, rel_bias):
    past_len = page_table.shape[1] * cache_a_kv.shape[2]

    def gather(cache, layer):
        g = cache[layer, page_table]
        return g.reshape((g.shape[0], past_len) + g.shape[3:])

    xp, xs = x_prompt, x_sample
    a_kv_p, a_kv_s, a_ki_p, a_ki_s = [], [], [], []
    b_kv_p, b_kv_s, c_kv_p, c_kv_s = [], [], [], []
    for l in range(DEPTH):
        i = l // 2
        hp = _rmsnorm(xp, ln_w[l])
        hs = _rmsnorm(xs, ln_w[l])
        if l % 2 == 0:
            yp, (akv, aki, bkv) = _ab_mixer(hp, None, 0, w_in_ab[i], w_out_ab[i], qn_a[i], kn_a[i], rel_bias)
            a_kv_p.append(akv)
            a_ki_p.append(aki)
            b_kv_p.append(bkv)
            past = (gather(cache_a_kv, i), gather(cache_a_kidx, i), gather(cache_b_kv, i))
            ys, (akv, aki, bkv) = _ab_mixer(hs, past, past_len, w_in_ab[i], w_out_ab[i], qn_a[i], kn_a[i], rel_bias)
            a_kv_s.append(akv)
            a_ki_s.append(aki)
            b_kv_s.append(bkv)
        else:
            yp, ckv = _c_mixer(hp, None, 0, w_in_c[i], w_out_c[i], qn_c[i], kn_c[i], rel_bias)
            c_kv_p.append(ckv)
            ys, ckv = _c_mixer(hs, gather(cache_c_kv, i), past_len, w_in_c[i], w_out_c[i], qn_c[i], kn_c[i], rel_bias)
            c_kv_s.append(ckv)
        xp = xp + yp
        xs = xs + ys
    return (xp, xs, jnp.stack(a_kv_p), jnp.stack(a_kv_s), jnp.stack(a_ki_p), jnp.stack(a_ki_s), jnp.stack(b_kv_p), jnp.stack(b_kv_s), jnp.stack(c_kv_p), jnp.stack(c_kv_s))
```

```python
import functools
import math

import numpy as np
import jax
import jax.numpy as jnp
from jax import lax
from jax.experimental import pallas as pl
from jax.experimental.pallas import tpu as pltpu

F32 = jnp.float32
HI = lax.Precision.HIGHEST
NEG_INF = float("-inf")
M_INIT = -1e30

D_MODEL = 1024
N_HEADS = 8
AB_DIM = 64
AB_WIDTH = N_HEADS * AB_DIM
C_DIM = 128
C_WIDTH = N_HEADS * C_DIM
IDX_HEADS = 8
IDX_DIM = 64
DSA_TOPK = 256
MOBA_BLOCK = 256
MOBA_TOPK = 3
PAGE = 128
REL_BUCKETS = 32
REL_MAX_DIST = 128
EPS = 1e-6

TQ = MOBA_BLOCK
GW = 512
TM = 256
NT = (((1,), (1,)), ((), ()))
VMEM_LIMIT = 48 * 1024 * 1024


def _dot(a, b):
    return jnp.dot(a, b, precision=HI, preferred_element_type=F32)


def _dot_nt(a, b):
    return lax.dot_general(a, b, NT, precision=HI, preferred_element_type=F32)


def _t5_bucket_np(dist):
    dist = np.maximum(np.asarray(dist, np.int64), 0)
    max_exact = REL_BUCKETS // 2
    d = np.maximum(dist, max_exact).astype(np.float64)
    val = np.log(d / max_exact) / math.log(REL_MAX_DIST / max_exact) * (REL_BUCKETS - max_exact)
    large = np.minimum(max_exact + val.astype(np.int64), REL_BUCKETS - 1)
    return np.where(dist < max_exact, dist, large).astype(np.int32)


def _prompt_bias(rel_bias):
    r = np.arange(TQ)[:, None]
    c = np.arange(2 * TQ)[None, :]
    band = jnp.transpose(rel_bias[_t5_bucket_np(TQ + r - c)], (2, 0, 1))
    far_bucket = _t5_bucket_np(np.array([TQ + 1, 1 << 20]))
    assert far_bucket[0] == far_bucket[1]
    far = jnp.broadcast_to(rel_bias[int(far_bucket[0])][:, None, None], (N_HEADS, 1, 128))
    return band.astype(F32), far.astype(F32)


def _sample_bias(rel_bias, past_len, t_new):
    n_pages = past_len // PAGE
    p = np.arange(n_pages + 1)[:, None, None]
    t = np.arange(t_new)[None, :, None]
    c = np.arange(PAGE)[None, None, :]
    dist = past_len + t - (p * PAGE + c)
    b = rel_bias[_t5_bucket_np(dist)]
    b = jnp.transpose(b, (0, 3, 1, 2)).reshape(n_pages + 1, N_HEADS * t_new, PAGE)
    return b.astype(F32)


def _head_mask(width, rows_per_head):
    d = width // N_HEADS
    h_row = np.arange(N_HEADS * rows_per_head)[:, None] // rows_per_head
    h_col = np.arange(width)[None, :] // d
    return jnp.asarray((h_row == h_col).astype(np.float32))


def _block_mean_matrix(width, d):
    g = np.arange(width) // d
    return jnp.asarray((g[:, None] == g[None, :]).astype(np.float32) / d)


def _proj_kernel(*refs, plan, n_main_out, has_small, has_kmean):
    it = iter(refs)
    x_ref, lnw_ref, w_ref, nw_ref, m_ref = (next(it) for _ in range(5))
    ws_ref = next(it) if has_small else None
    outs = [next(it) for _ in range(n_main_out)]
    ki_ref = next(it) if has_small else None
    small_ref = next(it) if has_small else None
    km_ref = next(it) if has_kmean else None
    h_ref = next(it)
    j = pl.program_id(1)

    @pl.when(j == 0)
    def _():
        x = x_ref[...]
        h = x * lax.rsqrt(jnp.mean(x * x, axis=-1, keepdims=True) + EPS) * lnw_ref[...]
        h_ref[...] = h
        if has_small:
            s = _dot(h, ws_ref[...])
            lane = lax.broadcasted_iota(jnp.int32, s.shape, 1)
            is_ki = lane < IDX_DIM
            ms = jnp.sum(jnp.where(is_ki, s * s, 0.0), axis=-1, keepdims=True) / IDX_DIM
            sm = jnp.where(is_ki, s * lax.rsqrt(ms + EPS), s * (IDX_HEADS ** -0.5))
            small_ref[...] = sm
            ki_ref[...] = sm[:, :IDX_DIM]

    acc = _dot(h_ref[...], w_ref[...])
    for jj, (oi, off, nrow, kmean_here) in enumerate(plan):
        @pl.when(j == jj)
        def _(oi=oi, off=off, nrow=nrow, kmean_here=kmean_here):
            val = acc
            if nrow is not None:
                ms = _dot(acc * acc, m_ref[...])
                val = acc * lax.rsqrt(ms + EPS) * nw_ref[nrow:nrow + 1, :]
            outs[oi][:, off:off + GW] = val
            if kmean_here and has_kmean:
                km_ref[0, :, off:off + GW] = jnp.sum(val, axis=0, keepdims=True) / MOBA_BLOCK


def _proj_call(x, ln_w, w_main, norm_w, mnorm, w_small, plan, out_widths, has_kmean):
    n = x.shape[0]
    tm = min(TM, n)
    assert n % tm == 0 and w_main.shape[1] == GW * len(plan)
    has_small = w_small is not None
    assert not has_kmean or tm == MOBA_BLOCK
    in_specs = [
        pl.BlockSpec((tm, D_MODEL), lambda i, j: (i, 0)),
        pl.BlockSpec((1, D_MODEL), lambda i, j: (0, 0)),
        pl.BlockSpec((D_MODEL, GW), lambda i, j: (0, j)),
        pl.BlockSpec(norm_w.shape, lambda i, j: (0, 0)),
        pl.BlockSpec((GW, GW), lambda i, j: (0, 0)),
    ]
    args = [x, ln_w.reshape(1, D_MODEL), w_main, norm_w, mnorm]
    if has_small:
        in_specs.append(pl.BlockSpec((D_MODEL, 128), lambda i, j: (0, 0)))
        args.append(w_small)
    out_shape = [jax.ShapeDtypeStruct((n, w), F32) for w in out_widths]
    out_specs = [pl.BlockSpec((tm, w), lambda i, j: (i, 0)) for w in out_widths]
    if has_small:
        out_shape += [jax.ShapeDtypeStruct((n, IDX_DIM), F32), jax.ShapeDtypeStruct((n, 128), F32)]
        out_specs += [pl.BlockSpec((tm, IDX_DIM), lambda i, j: (i, 0)),
                      pl.BlockSpec((tm, 128), lambda i, j: (i, 0))]
    if has_kmean:
        out_shape.append(jax.ShapeDtypeStruct((n // tm, 1, C_WIDTH), F32))
        out_specs.append(pl.BlockSpec((1, 1, C_WIDTH), lambda i, j: (i, 0, 0)))
    kern = functools.partial(_proj_kernel, plan=tuple(plan), n_main_out=len(out_widths),
                             has_small=has_small, has_kmean=has_kmean)
    return pl.pallas_call(
        kern,
        grid=(n // tm, len(plan)),
        in_specs=in_specs,
        out_specs=out_specs,
        out_shape=out_shape,
        scratch_shapes=[pltpu.VMEM((tm, D_MODEL), F32)],
        compiler_params=pltpu.CompilerParams(
            dimension_semantics=("parallel", "arbitrary"), vmem_limit_bytes=VMEM_LIMIT),
        name="proj_ab" if has_small else "proj_c",
    )(*args)


def _ab_weights(w_in, qn, kn):
    offs = np.cumsum([0, 512, 512, 512, 512, 512, 64, 8, 512, 512, 512, 512])
    qa, ka, va, ga, qi, ki, wi, qb, kb, vb, gb = (w_in[:, offs[k]:offs[k + 1]] for k in range(11))
    w_main = jnp.concatenate([ka, va, kb, vb, ga, gb, qa, qi, qb], axis=1)
    w_small = jnp.concatenate([ki, wi, jnp.zeros((D_MODEL, 128 - IDX_DIM - IDX_HEADS), F32)], axis=1)
    norm_w = jnp.stack([jnp.tile(kn, N_HEADS), jnp.tile(qn, N_HEADS)])
    return w_main, w_small, norm_w


_AB_PLAN = [(0, 0, 0, False), (0, 512, None, False),
            (1, 0, None, False), (1, 512, None, False),
            (2, 0, None, False), (2, 512, None, False),
            (3, 0, 1, False),
            (4, 0, None, False),
            (5, 0, None, False)]
_AB_WIDTHS = [1024, 1024, 1024, 512, 512, 512]

_C_PLAN = [(0, 0, 0, True), (0, 512, 0, True),
           (0, 1024, None, False), (0, 1536, None, False),
           (1, 0, None, False), (1, 512, None, False),
           (2, 0, 1, False), (2, 512, 1, False)]
_C_WIDTHS = [2048, 1024, 1024]


def _c_weights(w_in, qn, kn):
    q, k, v, g = (w_in[:, c * C_WIDTH:(c + 1) * C_WIDTH] for c in range(4))
    w_main = jnp.concatenate([k, v, g, q], axis=1)
    norm_w = jnp.stack([jnp.tile(kn, GW // C_DIM), jnp.tile(qn, GW // C_DIM)])
    return w_main, norm_w


def _out_kernel(*refs, n_o):
    x_ref, g_ref = refs[0], refs[1]
    o_refs = refs[2:2 + n_o]
    w_ref, y_ref = refs[2 + n_o], refs[3 + n_o]
    g = g_ref[...]
    sg = g * (1.0 / (1.0 + jnp.exp(-g)))
    y = x_ref[...]
    off = 0
    for o_ref in o_refs:
        w = o_ref.shape[1]
        y = y + _dot(o_ref[...] * sg[:, off:off + w], w_ref[off:off + w, :])
        off += w
    y_ref[...] = y


def _out_call(x, g, os_, w_out):
    n = x.shape[0]
    tm = min(TM, n)
    assert n % tm == 0
    in_specs = [pl.BlockSpec((tm, D_MODEL), lambda i: (i, 0)),
                pl.BlockSpec((tm, D_MODEL), lambda i: (i, 0))]
    in_specs += [pl.BlockSpec((tm, o.shape[1]), lambda i: (i, 0)) for o in os_]
    in_specs.append(pl.BlockSpec((D_MODEL, D_MODEL), lambda i: (0, 0)))
    return pl.pallas_call(
        functools.partial(_out_kernel, n_o=len(os_)),
        grid=(n // tm,),
        in_specs=in_specs,
        out_specs=pl.BlockSpec((tm, D_MODEL), lambda i: (i, 0)),
        out_shape=jax.ShapeDtypeStruct((n, D_MODEL), F32),
        compiler_params=pltpu.CompilerParams(
            dimension_semantics=("parallel",), vmem_limit_bytes=VMEM_LIMIT),
        name="out_proj",
    )(x, g, *os_, w_out)


def _rsum(x):
    return jnp.sum(jnp.sum(x, axis=0), axis=-1, keepdims=True)


def _rmin(x):
    return jnp.min(jnp.min(x, axis=0), axis=-1, keepdims=True)


def _rmax(x):
    return jnp.max(jnp.max(x, axis=0), axis=-1, keepdims=True)


def _topk_to_mask(sc_ref, key_index, k):
    kf = float(k)
    ones = lambda m: jnp.where(m, 1.0, 0.0)
    s0 = sc_ref[...]
    valid0 = s0 > NEG_INF
    nvalid = _rsum(ones(valid0))
    a0 = _rmin(jnp.where(valid0, s0, jnp.inf))
    done0 = ones(nvalid <= kf)
    state0 = (a0, jnp.full_like(a0, jnp.inf), jnp.full_like(a0, NEG_INF), done0)

    def cond(st):
        return jnp.min(st[3]) < 0.5

    def body(st):
        a, bx, v, done = st
        s = sc_ref[...]
        cand = (s >= a[None]) & (s < bx[None])
        mn = _rmin(jnp.where(cand, s, jnp.inf))
        mx = _rmax(jnp.where(cand, s, NEG_INF))
        p = mn * 0.5 + mx * 0.5
        p = jnp.where(p > mn, p, mx)
        c = _rsum(ones(s >= p[None]))
        tie = mn >= mx
        hit = c == kf
        is_done = done > 0.5
        stop = is_done | tie | hit
        v = jnp.where(is_done, v, jnp.where(tie, mn, jnp.where(hit, p, v)))
        a = jnp.where(stop, a, jnp.where(c > kf, p, a))
        bx = jnp.where(stop, bx, jnp.where(c < kf, p, bx))
        return a, bx, v, ones(stop)

    _, _, v, _ = lax.while_loop(cond, body, state0)

    s = sc_ref[...]
    gt = s > v[None]
    eq = s == v[None]
    need = kf - _rsum(ones(gt))
    excess = (_rsum(ones(eq)) > need) & (v > NEG_INF)
    has_ties = jnp.max(ones(excess)) > 0.5

    @pl.when(has_ties)
    def _():
        s = sc_ref[...]
        gt = s > v[None]
        eq = s == v[None]
        kid = key_index()
        nbits = (s.shape[0] * s.shape[2]).bit_length()
        cut = jnp.zeros(v.shape, jnp.int32)
        for bit in reversed(range(nbits)):
            cnd = cut + (1 << bit)
            cn = _rsum(ones(eq & (kid < cnd[None])))
            cut = jnp.where(cn <= need, cnd, cut)
        sel = (gt | (eq & (kid < cut[None]))) & (s > NEG_INF)
        sc_ref[...] = jnp.where(sel, 0.0, NEG_INF)

    @pl.when(jnp.logical_not(has_ties))
    def _():
        s = sc_ref[...]
        sel = (s >= v[None]) & (s > NEG_INF)
        sc_ref[...] = jnp.where(sel, 0.0, NEG_INF)


def _softmax_step(lg, vh, m, l, acc):
    m_new = jnp.maximum(m, jnp.max(lg, axis=-1, keepdims=True))
    alpha = jnp.exp(m - m_new)
    p = jnp.exp(lg - m_new)
    l = alpha * l + jnp.sum(p, axis=-1, keepdims=True)
    acc = alpha * acc + _dot(p, vh)
    return m_new, l, acc


def _tile_bias(band_h, far_h, j, i):
    return jnp.where(j == i, band_h[:, TQ:], jnp.where(j == i - 1, band_h[:, :TQ], far_h))


def _dsa_prompt_kernel(qa_ref, qi_ref, sm_ref, kv_ref, ki_ref, band_ref, far_ref, o_ref, sc_ref, *, topk):
    i = pl.program_id(1)
    nk = sc_ref.shape[0]
    row = lax.broadcasted_iota(jnp.int32, (TQ, TQ), 0)
    col = lax.broadcasted_iota(jnp.int32, (TQ, TQ), 1)
    sc_ref[...] = jnp.full(sc_ref.shape, NEG_INF, F32)
    wi = sm_ref[:, IDX_DIM:IDX_DIM + IDX_HEADS]

    def score_body(j, carry):
        kid = ki_ref[pl.ds(pl.multiple_of(j * TQ, TQ), TQ), :]
        sc = jnp.zeros((TQ, TQ), F32)
        for hh in range(IDX_HEADS):
            s = _dot_nt(qi_ref[:, hh * IDX_DIM:(hh + 1) * IDX_DIM], kid) * (IDX_DIM ** -0.5)
            sc = sc + wi[:, hh:hh + 1] * jnp.maximum(s, 0.0)
        causal = (col + j * TQ) <= (row + i * TQ)
        sc_ref[j] = jnp.where(causal, sc, NEG_INF)
        return carry

    lax.fori_loop(0, i + 1, score_body, 0)

    def key_index():
        shape = (nk, TQ, TQ)
        return lax.broadcasted_iota(jnp.int32, shape, 0) * TQ + lax.broadcasted_iota(jnp.int32, shape, 2)

    _topk_to_mask(sc_ref, key_index, topk)

    for h in range(N_HEADS):
        qh = qa_ref[:, h * AB_DIM:(h + 1) * AB_DIM]
        band_h = band_ref[h]
        far_h = far_ref[h][:, :1]

        def body(j, carry, h=h, qh=qh, band_h=band_h, far_h=far_h):
            rows = pl.ds(pl.multiple_of(j * TQ, TQ), TQ)
            kh = kv_ref[rows, h * AB_DIM:(h + 1) * AB_DIM]
            vh = kv_ref[rows, AB_WIDTH + h * AB_DIM:AB_WIDTH + (h + 1) * AB_DIM]
            lg = _dot_nt(qh, kh) * (AB_DIM ** -0.5) + _tile_bias(band_h, far_h, j, i) + sc_ref[j]
            return _softmax_step(lg, vh, *carry)

        init = (jnp.full((TQ, 1), M_INIT, F32), jnp.zeros((TQ, 1), F32), jnp.zeros((TQ, AB_DIM), F32))
        _, l, acc = lax.fori_loop(0, i + 1, body, init)
        o_ref[:, h * AB_DIM:(h + 1) * AB_DIM] = acc / l


def _dsa_prompt(qa, qi, small, kv_a, ki, band, far, batch, seq):
    nq = seq // TQ
    topk = min(DSA_TOPK, seq // 4)
    row_tile = lambda b, i: (b * nq + i, 0)
    whole = lambda b, i: (b, 0)
    return pl.pallas_call(
        functools.partial(_dsa_prompt_kernel, topk=topk),
        grid=(batch, nq),
        in_specs=[pl.BlockSpec((TQ, AB_WIDTH), row_tile),
                  pl.BlockSpec((TQ, AB_WIDTH), row_tile),
                  pl.BlockSpec((TQ, 128), row_tile),
                  pl.BlockSpec((seq, 2 * AB_WIDTH), whole),
                  pl.BlockSpec((seq, IDX_DIM), whole),
                  pl.BlockSpec(band.shape, lambda b, i: (0, 0, 0)),
                  pl.BlockSpec(far.shape, lambda b, i: (0, 0, 0))],
        out_specs=pl.BlockSpec((TQ, AB_WIDTH), row_tile),
        out_shape=jax.ShapeDtypeStruct((batch * seq, AB_WIDTH), F32),
        scratch_shapes=[pltpu.VMEM((nq, TQ, TQ), F32)],
        compiler_params=pltpu.CompilerParams(
            dimension_semantics=("parallel", "arbitrary"), vmem_limit_bytes=VMEM_LIMIT),
        name="dsa_prompt",
    )(qa, qi, small, kv_a, ki, band, far)


def _softplus_parts(z):
    t = jnp.log1p(jnp.exp(-jnp.abs(z)))
    return -(jnp.maximum(-z, 0.0) + t), -(jnp.maximum(z, 0.0) + t)


def _sb_prompt_kernel(q_ref, kv_ref, u_ref, o_ref):
    i = pl.program_id(1)
    row = lax.broadcasted_iota(jnp.int32, (TQ, TQ), 0)
    col = lax.broadcasted_iota(jnp.int32, (TQ, TQ), 1)
    u = u_ref[...]
    for h in range(N_HEADS):
        qh = q_ref[:, h * AB_DIM:(h + 1) * AB_DIM]

        def body(t, carry, h=h, qh=qh):
            rs, acc = carry
            j = i - t
            rows = pl.ds(pl.multiple_of(j * TQ, TQ), TQ)
            kh = kv_ref[rows, h * AB_DIM:(h + 1) * AB_DIM]
            vh = kv_ref[rows, AB_WIDTH + h * AB_DIM:AB_WIDTH + (h + 1) * AB_DIM]
            z = _dot_nt(qh, kh) * (AB_DIM ** -0.5)
            mask = (col + j * TQ) < (row + i * TQ)
            log_beta, log_1m = _softplus_parts(z)
            log_1m = jnp.where(mask, log_1m, 0.0)
            suffix = _dot(log_1m, u) + rs
            w = jnp.where(mask, jnp.exp(log_beta + suffix), 0.0)
            return rs + jnp.sum(log_1m, axis=-1, keepdims=True), acc + _dot(w, vh)

        init = (jnp.zeros((TQ, 1), F32), jnp.zeros((TQ, AB_DIM), F32))
        _, acc = lax.fori_loop(0, i + 1, body, init)
        o_ref[:, h * AB_DIM:(h + 1) * AB_DIM] = acc


def _strict_lower_ones(n):
    idx = np.arange(n)
    return jnp.asarray((idx[:, None] > idx[None, :]).astype(np.float32))


def _sb_prompt(qb, kv_b, batch, seq):
    nq = seq // TQ
    row_tile = lambda b, i: (b * nq + i, 0)
    return pl.pallas_call(
        _sb_prompt_kernel,
        grid=(batch, nq),
        in_specs=[pl.BlockSpec((TQ, AB_WIDTH), row_tile),
                  pl.BlockSpec((seq, 2 * AB_WIDTH), lambda b, i: (b, 0)),
                  pl.BlockSpec((TQ, TQ), lambda b, i: (0, 0))],
        out_specs=pl.BlockSpec((TQ, AB_WIDTH), row_tile),
        out_shape=jax.ShapeDtypeStruct((batch * seq, AB_WIDTH), F32),
        compiler_params=pltpu.CompilerParams(
            dimension_semantics=("parallel", "arbitrary"), vmem_limit_bytes=VMEM_LIMIT),
        name="sb_prompt",
    )(qb, kv_b, _strict_lower_ones(TQ))


def _moba_select(gate, n_valid):
    nb = gate.shape[1]
    blk = lax.broadcasted_iota(jnp.int32, gate.shape, 1)
    gate = jnp.where(blk < n_valid, gate, NEG_INF)
    sel = jnp.zeros(gate.shape, F32)
    for n in range(nb):
        gn = gate[:, n:n + 1]
        beats = (gate > gn) | ((gate == gn) & (blk < n))
        rank = jnp.sum(jnp.where(beats, 1.0, 0.0), axis=-1, keepdims=True)
        chosen = (rank < float(MOBA_TOPK)) & (blk == n) & (blk < n_valid)
        sel = jnp.where(chosen, 1.0, sel)
    return sel


def _moba_prompt_kernel(q_ref, k_ref, v_ref, km_ref, band_ref, far_ref, o_ref):
    i = pl.program_id(2)
    row = lax.broadcasted_iota(jnp.int32, (TQ, TQ), 0)
    col = lax.broadcasted_iota(jnp.int32, (TQ, TQ), 1)
    q = q_ref[...]
    gate = _dot_nt(q, km_ref[:, 0, :])
    sel = _moba_select(gate, i)
    blk = lax.broadcasted_iota(jnp.int32, sel.shape, 1)
    band = band_ref[...]
    far = far_ref[...][:, :1]

    def body(j, carry):
        rows = pl.ds(pl.multiple_of(j * TQ, TQ), TQ)
        lg = _dot_nt(q, k_ref[rows, :]) * (C_DIM ** -0.5) + _tile_bias(band, far, j, i)
        sel_j = jnp.sum(jnp.where(blk == j, sel, 0.0), axis=-1, keepdims=True)
        ok = jnp.where(j == i, jnp.where(col <= row, 1.0, 0.0), sel_j)
        lg = jnp.where(ok > 0.5, lg, NEG_INF)
        return _softmax_step(lg, v_ref[rows, :], *carry)

    init = (jnp.full((TQ, 1), M_INIT, F32), jnp.zeros((TQ, 1), F32), jnp.zeros((TQ, C_DIM), F32))
    _, l, acc = lax.fori_loop(0, i + 1, body, init)
    o_ref[...] = acc / l


def _moba_prompt(q, kv_c, kmean, band, far, batch, seq):
    nq = seq // TQ
    assert TQ == MOBA_BLOCK
    return pl.pallas_call(
        _moba_prompt_kernel,
        grid=(batch, N_HEADS, nq),
        in_specs=[pl.BlockSpec((TQ, C_DIM), lambda b, h, i: (b * nq + i, h)),
                  pl.BlockSpec((seq, C_DIM), lambda b, h, i: (b, h)),
                  pl.BlockSpec((seq, C_DIM), lambda b, h, i: (b, N_HEADS + h)),
                  pl.BlockSpec((nq, 1, C_DIM), lambda b, h, i: (b, 0, h)),
                  pl.BlockSpec((None, TQ, 2 * TQ), lambda b, h, i: (h, 0, 0)),
                  pl.BlockSpec((None, 1, 128), lambda b, h, i: (h, 0, 0))],
        out_specs=pl.BlockSpec((TQ, C_DIM), lambda b, h, i: (b * nq + i, h)),
        out_shape=jax.ShapeDtypeStruct((batch * seq, C_WIDTH), F32),
        compiler_params=pltpu.CompilerParams(
            dimension_semantics=("parallel", "parallel", "arbitrary"), vmem_limit_bytes=VMEM_LIMIT),
        name="moba_prompt",
    )(q, kv_c, kv_c, kmean, band, far)


def _page_specs(n_pages, width, layer):
    return [pl.BlockSpec((None, None, PAGE, width), lambda b, pt, p=p: (layer, pt[b, p], 0, 0))
            for p in range(n_pages)]


def _pad_rows(x, rows):
    return jnp.concatenate([x, jnp.zeros((rows - x.shape[0], x.shape[1]), x.dtype)], axis=0)


def _head_rows(q, hm):
    return jnp.concatenate([q] * N_HEADS, axis=0) * hm


def _head_diag(res, hm, t):
    out = res[0:t, :] * hm[0:t, :]
    for h in range(1, N_HEADS):
        out = out + res[h * t:(h + 1) * t, :] * hm[h * t:(h + 1) * t, :]
    return out


def _softmax_pages(lg_ref, values, t):
    n = lg_ref.shape[0]
    m = jnp.max(lg_ref[0], axis=-1, keepdims=True)
    for p in range(1, n):
        m = jnp.maximum(m, jnp.max(lg_ref[p], axis=-1, keepdims=True))
    l = jnp.zeros_like(m)
    acc = None
    for p in range(n):
        e = jnp.exp(lg_ref[p] - m)
        l = l + jnp.sum(e, axis=-1, keepdims=True)
        pv = _dot(e, values(p))
        acc = pv if acc is None else acc + pv
    return acc / l


def _dsa_sample_kernel(pt_ref, qa_ref, qir_ref, sm_ref, kvn_ref, kin_ref, hm_ref, bias_ref, *rest,
                       n_pages, t_new, topk):
    kv_pages = rest[:n_pages]
    ki_pages = rest[n_pages:2 * n_pages]
    o_ref, sc_ref, lg_ref = rest[2 * n_pages:]
    wi = sm_ref[:, IDX_DIM:IDX_DIM + IDX_HEADS]
    qir = qir_ref[...]
    lane = lax.broadcasted_iota(jnp.int32, (t_new, PAGE), 1)
    trow = lax.broadcasted_iota(jnp.int32, (t_new, PAGE), 0)

    def page_score(kid):
        r = jnp.maximum(_dot_nt(qir, kid) * (IDX_DIM ** -0.5), 0.0)
        sc = jnp.zeros((t_new, PAGE), F32)
        for hh in range(IDX_HEADS):
            sc = sc + wi[:, hh:hh + 1] * r[hh * t_new:(hh + 1) * t_new, :]
        return sc

    for p in range(n_pages):
        sc_ref[p] = page_score(ki_pages[p][...])
    sc_ref[n_pages] = jnp.where(lane <= trow, page_score(_pad_rows(kin_ref[...], PAGE)), NEG_INF)

    def key_index():
        shape = (n_pages + 1, t_new, PAGE)
        return lax.broadcasted_iota(jnp.int32, shape, 0) * PAGE + lax.broadcasted_iota(jnp.int32, shape, 2)

    _topk_to_mask(sc_ref, key_index, topk)

    hm = hm_ref[...]
    qrows = _head_rows(qa_ref[...], hm)
    kvn = _pad_rows(kvn_ref[...], PAGE)
    for p in range(n_pages + 1):
        k = kv_pages[p][:, :AB_WIDTH] if p < n_pages else kvn[:, :AB_WIDTH]
        mask = jnp.concatenate([sc_ref[p]] * N_HEADS, axis=0)
        lg_ref[p] = _dot_nt(qrows, k) * (AB_DIM ** -0.5) + bias_ref[p] + mask

    values = lambda p: kv_pages[p][:, AB_WIDTH:] if p < n_pages else kvn[:, AB_WIDTH:]
    o_ref[...] = _head_diag(_softmax_pages(lg_ref, values, t_new), hm, t_new)


def _sample_common_specs(t_new, width):
    return pl.BlockSpec((t_new, width), lambda b, pt: (b, 0))


def _dsa_sample(page_table, qa, qi_rows, small, kv_new, ki_new, hm, bias, cache_kv, cache_ki, layer, t_new):
    batch, n_pages = page_table.shape
    topk = min(DSA_TOPK, (n_pages * PAGE + t_new) // 4)
    const2 = lambda b, pt: (0, 0)
    in_specs = [_sample_common_specs(t_new, AB_WIDTH),
                pl.BlockSpec((None, IDX_HEADS * t_new, IDX_DIM), lambda b, pt: (b, 0, 0)),
                _sample_common_specs(t_new, 128),
                _sample_common_specs(t_new, 2 * AB_WIDTH),
                _sample_common_specs(t_new, IDX_DIM),
                pl.BlockSpec(hm.shape, const2),
                pl.BlockSpec(bias.shape, lambda b, pt: (0, 0, 0))]
    in_specs += _page_specs(n_pages, 2 * AB_WIDTH, layer) + _page_specs(n_pages, IDX_DIM, layer)
    grid_spec = pltpu.PrefetchScalarGridSpec(
        num_scalar_prefetch=1, grid=(batch,), in_specs=in_specs,
        out_specs=_sample_common_specs(t_new, AB_WIDTH),
        scratch_shapes=[pltpu.VMEM((n_pages + 1, t_new, PAGE), F32),
                        pltpu.VMEM((n_pages + 1, N_HEADS * t_new, PAGE), F32)])
    return pl.pallas_call(
        functools.partial(_dsa_sample_kernel, n_pages=n_pages, t_new=t_new, topk=topk),
        grid_spec=grid_spec,
        out_shape=jax.ShapeDtypeStruct((batch * t_new, AB_WIDTH), F32),
        compiler_params=pltpu.CompilerParams(
            dimension_semantics=("arbitrary",), vmem_limit_bytes=VMEM_LIMIT),
        name="dsa_sample",
    )(page_table, qa, qi_rows, small, kv_new, ki_new, hm, bias,
      *([cache_kv] * n_pages), *([cache_ki] * n_pages))


def _sb_sample_kernel(pt_ref, q_ref, kvn_ref, hm_ref, u_ref, *rest, n_pages, t_new):
    kv_pages = rest[:n_pages]
    o_ref = rest[n_pages]
    hm = hm_ref[...]
    u = u_ref[...]
    qrows = _head_rows(q_ref[...], hm)
    kvn = _pad_rows(kvn_ref[...], PAGE)
    shape = (N_HEADS * t_new, PAGE)
    lane = lax.broadcasted_iota(jnp.int32, shape, 1)
    trow = lax.rem(lax.broadcasted_iota(jnp.int32, shape, 0), t_new)
    rs = jnp.zeros((N_HEADS * t_new, 1), F32)
    acc = jnp.zeros((N_HEADS * t_new, AB_WIDTH), F32)
    for p in reversed(range(n_pages + 1)):
        kv = kv_pages[p][...] if p < n_pages else kvn
        z = _dot_nt(qrows, kv[:, :AB_WIDTH]) * (AB_DIM ** -0.5)
        log_beta, log_1m = _softplus_parts(z)
        if p == n_pages:
            mask = lane < trow
            log_1m = jnp.where(mask, log_1m, 0.0)
        suffix = _dot(log_1m, u) + rs
        w = jnp.exp(log_beta + suffix)
        if p == n_pages:
            w = jnp.where(mask, w, 0.0)
        acc = acc + _dot(w, kv[:, AB_WIDTH:])
        rs = rs + jnp.sum(log_1m, axis=-1, keepdims=True)
    o_ref[...] = _head_diag(acc, hm, t_new)


def _sb_sample(page_table, qb, kv_new, hm, cache_kv, layer, t_new):
    batch, n_pages = page_table.shape
    in_specs = [_sample_common_specs(t_new, AB_WIDTH),
                _sample_common_specs(t_new, 2 * AB_WIDTH),
                pl.BlockSpec(hm.shape, lambda b, pt: (0, 0)),
                pl.BlockSpec((PAGE, PAGE), lambda b, pt: (0, 0))]
    in_specs += _page_specs(n_pages, 2 * AB_WIDTH, layer)
    grid_spec = pltpu.PrefetchScalarGridSpec(
        num_scalar_prefetch=1, grid=(batch,), in_specs=in_specs,
        out_specs=_sample_common_specs(t_new, AB_WIDTH))
    return pl.pallas_call(
        functools.partial(_sb_sample_kernel, n_pages=n_pages, t_new=t_new),
        grid_spec=grid_spec,
        out_shape=jax.ShapeDtypeStruct((batch * t_new, AB_WIDTH), F32),
        compiler_params=pltpu.CompilerParams(
            dimension_semantics=("arbitrary",), vmem_limit_bytes=VMEM_LIMIT),
        name="sb_sample",
    )(page_table, qb, kv_new, hm, _strict_lower_ones(PAGE), *([cache_kv] * n_pages))


def _moba_sample_kernel(pt_ref, q_ref, kvn_ref, hm_ref, bias_ref, *rest, n_pages, t_new):
    kv_pages = rest[:n_pages]
    o_ref, km_ref, lg_ref = rest[n_pages:]
    pages_per_block = MOBA_BLOCK // PAGE
    n_blocks = n_pages // pages_per_block
    hm = hm_ref[...]
    qrows = _head_rows(q_ref[...], hm)
    kvn = _pad_rows(kvn_ref[...], PAGE)
    for n in range(n_blocks):
        tot = jnp.sum(kv_pages[n * pages_per_block][:, :C_WIDTH], axis=0, keepdims=True)
        for r in range(1, pages_per_block):
            tot = tot + jnp.sum(kv_pages[n * pages_per_block + r][:, :C_WIDTH], axis=0, keepdims=True)
        km_ref[n:n + 1, :] = tot / MOBA_BLOCK
    gate = _dot_nt(qrows, km_ref[...])
    sel = _moba_select(gate, n_blocks)
    shape = (N_HEADS * t_new, PAGE)
    lane = lax.broadcasted_iota(jnp.int32, shape, 1)
    trow = lax.rem(lax.broadcasted_iota(jnp.int32, shape, 0), t_new)
    for p in range(n_pages + 1):
        k = kv_pages[p][:, :C_WIDTH] if p < n_pages else kvn[:, :C_WIDTH]
        if p < n_pages:
            n = p // pages_per_block
            ok = sel[:, n:n + 1] > 0.5
        else:
            ok = lane <= trow
        lg = _dot_nt(qrows, k) * (C_DIM ** -0.5) + bias_ref[p]
        lg_ref[p] = jnp.where(ok, lg, NEG_INF)
    values = lambda p: kv_pages[p][:, C_WIDTH:] if p < n_pages else kvn[:, C_WIDTH:]
    o_ref[...] = _head_diag(_softmax_pages(lg_ref, values, t_new), hm, t_new)


def _moba_sample(page_table, q, kv_new, hm, bias, cache_kv, layer, t_new):
    batch, n_pages = page_table.shape
    assert (n_pages * PAGE) % MOBA_BLOCK == 0 and t_new <= MOBA_BLOCK
    in_specs = [_sample_common_specs(t_new, C_WIDTH),
                _sample_common_specs(t_new, 2 * C_WIDTH),
                pl.BlockSpec(hm.shape, lambda b, pt: (0, 0)),
                pl.BlockSpec(bias.shape, lambda b, pt: (0, 0, 0))]
    in_specs += _page_specs(n_pages, 2 * C_WIDTH, layer)
    grid_spec = pltpu.PrefetchScalarGridSpec(
        num_scalar_prefetch=1, grid=(batch,), in_specs=in_specs,
        out_specs=_sample_common_specs(t_new, C_WIDTH),
        scratch_shapes=[pltpu.VMEM((n_pages * PAGE // MOBA_BLOCK, C_WIDTH), F32),
                        pltpu.VMEM((n_pages + 1, N_HEADS * t_new, PAGE), F32)])
    return pl.pallas_call(
        functools.partial(_moba_sample_kernel, n_pages=n_pages, t_new=t_new),
        grid_spec=grid_spec,
        out_shape=jax.ShapeDtypeStruct((batch * t_new, C_WIDTH), F32),
        compiler_params=pltpu.CompilerParams(
            dimension_semantics=("arbitrary",), vmem_limit_bytes=VMEM_LIMIT),
        name="moba_sample",
    )(page_table, q, kv_new, hm, bias, *([cache_kv] * n_pages))


def kernel(x_prompt, x_sample, cache_a_kv, cache_a_kidx, cache_b_kv, cache_c_kv, page_table, ln_w,
           w_in_ab, w_out_ab, qn_a, kn_a, w_in_c, w_out_c, qn_c, kn_c, rel_bias):
    batch, seq, _ = x_prompt.shape
    dec_batch, t_new, _ = x_sample.shape
    n_pages = page_table.shape[1]
    past_len = n_pages * PAGE
    depth = ln_w.shape[0]
    assert cache_a_kv.shape[2] == PAGE and seq % TQ == 0

    xp = x_prompt.reshape(batch * seq, D_MODEL)
    xs = x_sample.reshape(dec_batch * t_new, D_MODEL)
    n_pool = cache_a_kv.shape[1]
    ca_kv = cache_a_kv.reshape(-1, n_pool, PAGE, 2 * AB_WIDTH)
    cb_kv = cache_b_kv.reshape(-1, n_pool, PAGE, 2 * AB_WIDTH)
    cc_kv = cache_c_kv.reshape(-1, n_pool, PAGE, 2 * C_WIDTH)

    band, far = _prompt_bias(rel_bias)
    bias_s = _sample_bias(rel_bias, past_len, t_new)
    hm_ab = _head_mask(AB_WIDTH, t_new)
    hm_c = _head_mask(C_WIDTH, t_new)
    mnorm_ab = _block_mean_matrix(GW, AB_DIM)
    mnorm_c = _block_mean_matrix(GW, C_DIM)

    outs = {k: [] for k in ("a_kv_p", "a_kv_s", "a_ki_p", "a_ki_s", "b_kv_p", "b_kv_s", "c_kv_p", "c_kv_s")}
    for l in range(depth):
        i = l // 2
        if l % 2 == 0:
            w_main, w_small, norm_w = _ab_weights(w_in_ab[i], qn_a[i], kn_a[i])
            proj = lambda x: _proj_call(x, ln_w[l], w_main, norm_w, mnorm_ab, w_small,
                                        _AB_PLAN, _AB_WIDTHS, False)
            kv_a, kv_b, g, qa, qi, qb, ki, small = proj(xp)
            oa = _dsa_prompt(qa, qi, small, kv_a, ki, band, far, batch, seq)
            ob = _sb_prompt(qb, kv_b, batch, seq)
            xp = _out_call(xp, g, [oa, ob], w_out_ab[i])
            outs["a_kv_p"].append(kv_a)
            outs["a_ki_p"].append(ki)
            outs["b_kv_p"].append(kv_b)

            kv_a, kv_b, g, qa, qi, qb, ki, small = proj(xs)
            qi_rows = qi.reshape(dec_batch, t_new, IDX_HEADS, IDX_DIM).transpose(0, 2, 1, 3)
            qi_rows = qi_rows.reshape(dec_batch, IDX_HEADS * t_new, IDX_DIM)
            oa = _dsa_sample(page_table, qa, qi_rows, small, kv_a, ki, hm_ab, bias_s,
                             ca_kv, cache_a_kidx, i, t_new)
            ob = _sb_sample(page_table, qb, kv_b, hm_ab, cb_kv, i, t_new)
            xs = _out_call(xs, g, [oa, ob], w_out_ab[i])
            outs["a_kv_s"].append(kv_a)
            outs["a_ki_s"].append(ki)
            outs["b_kv_s"].append(kv_b)
        else:
            w_main, norm_w = _c_weights(w_in_c[i], qn_c[i], kn_c[i])
            proj = lambda x, km: _proj_call(x, ln_w[l], w_main, norm_w, mnorm_c, None,
                                            _C_PLAN, _C_WIDTHS, km)
            kv_c, g, q, kmean = proj(xp, True)
            o = _moba_prompt(q, kv_c, kmean, band, far, batch, seq)
            xp = _out_call(xp, g, [o], w_out_c[i])
            outs["c_kv_p"].append(kv_c)

            kv_c, g, q = proj(xs, False)
            o = _moba_sample(page_table, q, kv_c, hm_c, bias_s, cc_kv, i, t_new)
            xs = _out_call(xs, g, [o], w_out_c[i])
            outs["c_kv_s"].append(kv_c)

    def kv(name, b, t, d):
        return jnp.stack(outs[name]).reshape(-1, b, t, 2, N_HEADS, d)

    return (xp.reshape(batch, seq, D_MODEL), xs.reshape(dec_batch, t_new, D_MODEL),
            kv("a_kv_p", batch, seq, AB_DIM), kv("a_kv_s", dec_batch, t_new, AB_DIM),
            jnp.stack(outs["a_ki_p"]).reshape(-1, batch, seq, IDX_DIM),
            jnp.stack(outs["a_ki_s"]).reshape(-1, dec_batch, t_new, IDX_DIM),
            kv("b_kv_p", batch, seq, AB_DIM), kv("b_kv_s", dec_batch, t_new, AB_DIM),
            kv("c_kv_p", batch, seq, C_DIM), kv("c_kv_s", dec_batch, t_new, C_DIM))
```

```python
import functools
import math

import numpy as np
import jax
import jax.numpy as jnp
from jax import lax
from jax.experimental import pallas as pl
from jax.experimental.pallas import tpu as pltpu

F32 = jnp.float32
BF16 = jnp.bfloat16
HI = lax.Precision.HIGHEST
NEG_INF = float("-inf")
M_INIT = -1e30

D_MODEL = 1024
N_HEADS = 8
AB_DIM = 64
AB_WIDTH = N_HEADS * AB_DIM
C_DIM = 128
C_WIDTH = N_HEADS * C_DIM
IDX_HEADS = 8
IDX_DIM = 64
DSA_TOPK = 256
MOBA_BLOCK = 256
MOBA_TOPK = 3
PAGE = 128
REL_BUCKETS = 32
REL_MAX_DIST = 128
EPS = 1e-6

LANES = 128
TQ = MOBA_BLOCK
GW = 512
TM = 512
SMALL_W = LANES
WI_LO, WI_HI = IDX_DIM, IDX_DIM + IDX_HEADS
NN = (((1,), (0,)), ((), ()))
NT = (((1,), (1,)), ((), ()))
VMEM_LIMIT = 48 * 1024 * 1024


def _split(x):
    hi = x.astype(BF16)
    return hi, (x - hi.astype(F32)).astype(BF16)


def _dg(a, b, dims):
    return lax.dot_general(a, b, dims, preferred_element_type=F32)


def _dot3(a, b, dims=NN):
    return _dg(a[0], b[0], dims) + _dg(a[0], b[1], dims) + _dg(a[1], b[0], dims)


def _dot2(a, b_exact, dims=NN):
    return _dg(a[0], b_exact, dims) + _dg(a[1], b_exact, dims)


def _dot1(a, b, dims=NN):
    return _dg(a.astype(BF16), b.astype(BF16), dims)


def _t5_bucket_np(dist):
    dist = np.maximum(np.asarray(dist, np.int64), 0)
    max_exact = REL_BUCKETS // 2
    d = np.maximum(dist, max_exact).astype(np.float64)
    val = np.log(d / max_exact) / math.log(REL_MAX_DIST / max_exact) * (REL_BUCKETS - max_exact)
    large = np.minimum(max_exact + val.astype(np.int64), REL_BUCKETS - 1)
    return np.where(dist < max_exact, dist, large).astype(np.int32)


def _bias_lookup(rel_bias, buckets):
    onehot = (jnp.asarray(buckets.reshape(-1))[:, None] == jnp.arange(REL_BUCKETS)[None, :]).astype(F32)
    out = jnp.dot(onehot, rel_bias.astype(F32), precision=HI)
    return out.T.reshape((N_HEADS,) + buckets.shape)


def _prompt_bias(rel_bias):
    r = np.arange(TQ)[:, None]
    c = np.arange(2 * TQ)[None, :]
    band = _bias_lookup(rel_bias, _t5_bucket_np(TQ + r - c))
    far_bucket = _t5_bucket_np(np.array([TQ + 1, 1 << 20]))
    assert far_bucket[0] == far_bucket[1]
    far = jnp.broadcast_to(rel_bias[int(far_bucket[0])].astype(F32)[:, None, None], (N_HEADS, 1, LANES))
    return band, far


def _sample_bias(rel_bias, past_len, t_new):
    n_pages = past_len // PAGE
    p = np.arange(n_pages + 1)[:, None, None]
    t = np.arange(t_new)[None, :, None]
    c = np.arange(PAGE)[None, None, :]
    b = _bias_lookup(rel_bias, _t5_bucket_np(past_len + t - (p * PAGE + c)))
    return jnp.transpose(b, (1, 0, 2, 3)).reshape(n_pages + 1, N_HEADS * t_new, PAGE)


def _head_mask(width, rows_per_head):
    d = width // N_HEADS
    h_row = np.arange(N_HEADS * rows_per_head)[:, None] // rows_per_head
    h_col = np.arange(width)[None, :] // d
    return jnp.asarray((h_row == h_col).astype(np.float32))


def _block_mean_matrix(width, d):
    g = np.arange(width) // d
    return jnp.asarray((g[:, None] == g[None, :]).astype(np.float32) / d).astype(BF16)


def _strict_lower_ones(n):
    idx = np.arange(n)
    return jnp.asarray((idx[:, None] > idx[None, :]).astype(np.float32)).astype(BF16)


def _proj_kernel(*refs, plan, n_main_out, has_small, has_kmean):
    it = iter(refs)
    x_ref, lnw_ref, whi_ref, wlo_ref, nw_ref, m_ref = (next(it) for _ in range(6))
    wshi_ref = next(it) if has_small else None
    wslo_ref = next(it) if has_small else None
    outs = [next(it) for _ in range(n_main_out)]
    ki_ref = next(it) if has_small else None
    small_ref = next(it) if has_small else None
    km_ref = next(it) if has_kmean else None
    hhi_ref, hlo_ref = next(it), next(it)
    j = pl.program_id(1)

    @pl.when(j == 0)
    def _():
        x = x_ref[...]
        h = x * lax.rsqrt(jnp.mean(x * x, axis=-1, keepdims=True) + EPS) * lnw_ref[...]
        hs = _split(h)
        hhi_ref[...] = hs[0]
        hlo_ref[...] = hs[1]
        if has_small:
            s = _dot3(hs, (wshi_ref[...], wslo_ref[...]))
            lane = lax.broadcasted_iota(jnp.int32, s.shape, 1)
            is_ki = lane < IDX_DIM
            ms = jnp.sum(jnp.where(is_ki, s * s, 0.0), axis=-1, keepdims=True) / IDX_DIM
            sm = jnp.where(is_ki, s * lax.rsqrt(ms + EPS), s * (IDX_HEADS ** -0.5))
            small_ref[...] = sm
            ki_ref[...] = sm[:, :IDX_DIM]

    acc = _dot3((hhi_ref[...], hlo_ref[...]), (whi_ref[...], wlo_ref[...]))
    for jj, (oi, off, nrow, kmean_here) in enumerate(plan):
        @pl.when(j == jj)
        def _(oi=oi, off=off, nrow=nrow, kmean_here=kmean_here):
            val = acc
            if nrow is not None:
                ms = _dot2(_split(acc * acc), m_ref[...])
                val = acc * lax.rsqrt(ms + EPS) * nw_ref[nrow:nrow + 1, :]
            outs[oi][:, off:off + GW] = val
            if kmean_here and has_kmean:
                for r in range(val.shape[0] // MOBA_BLOCK):
                    blk = val[r * MOBA_BLOCK:(r + 1) * MOBA_BLOCK, :]
                    km_ref[r, :, off:off + GW] = jnp.sum(blk, axis=0, keepdims=True) / MOBA_BLOCK


def _proj_call(x, ln_w, w_main, norm_w, mnorm, w_small, plan, out_widths, has_kmean):
    n = x.shape[0]
    tm = min(TM, n)
    assert n % tm == 0 and w_main[0].shape[1] == GW * len(plan)
    has_small = w_small is not None
    assert not has_kmean or tm % MOBA_BLOCK == 0
    row = lambda i, j: (i, 0)
    const = lambda i, j: (0, 0)
    in_specs = [
        pl.BlockSpec((tm, D_MODEL), row),
        pl.BlockSpec((1, D_MODEL), const),
        pl.BlockSpec((D_MODEL, GW), lambda i, j: (0, j)),
        pl.BlockSpec((D_MODEL, GW), lambda i, j: (0, j)),
        pl.BlockSpec(norm_w.shape, const),
        pl.BlockSpec((GW, GW), const),
    ]
    args = [x, ln_w.reshape(1, D_MODEL), w_main[0], w_main[1], norm_w, mnorm]
    if has_small:
        in_specs += [pl.BlockSpec((D_MODEL, SMALL_W), const)] * 2
        args += [w_small[0], w_small[1]]
    out_shape = [jax.ShapeDtypeStruct((n, w), F32) for w in out_widths]
    out_specs = [pl.BlockSpec((tm, w), row) for w in out_widths]
    if has_small:
        out_shape += [jax.ShapeDtypeStruct((n, IDX_DIM), F32), jax.ShapeDtypeStruct((n, SMALL_W), F32)]
        out_specs += [pl.BlockSpec((tm, IDX_DIM), row), pl.BlockSpec((tm, SMALL_W), row)]
    if has_kmean:
        out_shape.append(jax.ShapeDtypeStruct((n // MOBA_BLOCK, 1, C_WIDTH), F32))
        out_specs.append(pl.BlockSpec((tm // MOBA_BLOCK, 1, C_WIDTH), lambda i, j: (i, 0, 0)))
    kern = functools.partial(_proj_kernel, plan=tuple(plan), n_main_out=len(out_widths),
                             has_small=has_small, has_kmean=has_kmean)
    return pl.pallas_call(
        kern,
        grid=(n // tm, len(plan)),
        in_specs=in_specs,
        out_specs=out_specs,
        out_shape=out_shape,
        scratch_shapes=[pltpu.VMEM((tm, D_MODEL), BF16), pltpu.VMEM((tm, D_MODEL), BF16)],
        compiler_params=pltpu.CompilerParams(
            dimension_semantics=("parallel", "arbitrary"), vmem_limit_bytes=VMEM_LIMIT),
        name="proj_ab" if has_small else "proj_c",
    )(*args)


def _ab_weights(w_in, qn, kn):
    offs = np.cumsum([0, 512, 512, 512, 512, 512, 64, 8, 512, 512, 512, 512])
    qa, ka, va, ga, qi, ki, wi, qb, kb, vb, gb = (w_in[:, offs[k]:offs[k + 1]] for k in range(11))
    w_main = jnp.concatenate([ka, va, kb, vb, ga, gb, qa, qi, qb], axis=1)
    w_small = jnp.concatenate([ki, wi, jnp.zeros((D_MODEL, SMALL_W - IDX_DIM - IDX_HEADS), F32)], axis=1)
    norm_w = jnp.stack([jnp.tile(kn, N_HEADS), jnp.tile(qn, N_HEADS)])
    return _split(w_main), _split(w_small), norm_w


_AB_PLAN = [(0, 0, 0, False), (0, 512, None, False),
            (1, 0, None, False), (1, 512, None, False),
            (2, 0, None, False), (2, 512, None, False),
            (3, 0, 1, False),
            (4, 0, None, False),
            (5, 0, None, False)]
_AB_WIDTHS = [1024, 1024, 1024, 512, 512, 512]

_C_PLAN = [(0, 0, 0, True), (0, 512, 0, True),
           (0, 1024, None, False), (0, 1536, None, False),
           (1, 0, None, False), (1, 512, None, False),
           (2, 0, 1, False), (2, 512, 1, False)]
_C_WIDTHS = [2048, 1024, 1024]


def _c_weights(w_in, qn, kn):
    q, k, v, g = (w_in[:, c * C_WIDTH:(c + 1) * C_WIDTH] for c in range(4))
    w_main = jnp.concatenate([k, v, g, q], axis=1)
    norm_w = jnp.stack([jnp.tile(kn, GW // C_DIM), jnp.tile(qn, GW // C_DIM)])
    return _split(w_main), norm_w


def _out_kernel(*refs, n_o):
    x_ref, g_ref = refs[0], refs[1]
    o_refs = refs[2:2 + n_o]
    whi_ref, wlo_ref, y_ref = refs[2 + n_o:]
    g = g_ref[...]
    sg = g * (1.0 / (1.0 + jnp.exp(-g)))
    y = x_ref[...]
    off = 0
    for o_ref in o_refs:
        w = o_ref.shape[1]
        y = y + _dot3(_split(o_ref[...] * sg[:, off:off + w]),
                      (whi_ref[off:off + w, :], wlo_ref[off:off + w, :]))
        off += w
    y_ref[...] = y


def _out_call(x, g, os_, w_out):
    n = x.shape[0]
    tm = min(TM, n)
    assert n % tm == 0
    row = lambda i: (i, 0)
    in_specs = [pl.BlockSpec((tm, D_MODEL), row), pl.BlockSpec((tm, D_MODEL), row)]
    in_specs += [pl.BlockSpec((tm, o.shape[1]), row) for o in os_]
    in_specs += [pl.BlockSpec((D_MODEL, D_MODEL), lambda i: (0, 0))] * 2
    return pl.pallas_call(
        functools.partial(_out_kernel, n_o=len(os_)),
        grid=(n // tm,),
        in_specs=in_specs,
        out_specs=pl.BlockSpec((tm, D_MODEL), row),
        out_shape=jax.ShapeDtypeStruct((n, D_MODEL), F32),
        compiler_params=pltpu.CompilerParams(
            dimension_semantics=("parallel",), vmem_limit_bytes=VMEM_LIMIT),
        name="out_proj",
    )(x, g, *os_, *w_out)


def _fold_blocks(sc_ref, nblk, f, comb, init_val):
    r, w = sc_ref.shape[1], sc_ref.shape[2]

    def lanes(x):
        out = x[:, :LANES]
        for c in range(1, w // LANES):
            out = comb(out, x[:, c * LANES:(c + 1) * LANES])
        return out

    init = jnp.full((r, LANES), init_val, F32)
    if isinstance(nblk, int):
        acc = init
        for j in range(nblk):
            acc = comb(acc, lanes(f(sc_ref[j], j)))
        return acc
    return lax.fori_loop(0, nblk, lambda j, acc: comb(acc, lanes(f(sc_ref[j], j))), init)


def _for_blocks(nblk, body):
    if isinstance(nblk, int):
        for j in range(nblk):
            body(j)
    else:
        lax.fori_loop(0, nblk, lambda j, c: (body(j), c)[1], 0)


def _topk_to_mask(sc_ref, nblk, k):
    kf = float(k)
    w = sc_ref.shape[2]
    ones = lambda m: jnp.where(m, 1.0, 0.0)
    add, fmin, fmax = jnp.add, jnp.minimum, jnp.maximum
    rsum = lambda acc: jnp.sum(acc, axis=-1, keepdims=True)
    count = lambda pred: rsum(_fold_blocks(sc_ref, nblk, lambda s, j: ones(pred(s, j)), add, 0.0))

    nvalid = count(lambda s, j: s > NEG_INF)
    mn = jnp.min(_fold_blocks(sc_ref, nblk, lambda s, j: jnp.where(s > NEG_INF, s, jnp.inf), fmin, jnp.inf),
                 axis=-1, keepdims=True)
    mx = jnp.max(_fold_blocks(sc_ref, nblk, lambda s, j: s, fmax, NEG_INF), axis=-1, keepdims=True)
    cmx = count(lambda s, j: s >= mx)
    few = nvalid <= kf
    v0 = jnp.where(few, NEG_INF, mx)
    done0 = ones(few | (cmx >= kf))

    def cond(st):
        return jnp.min(st[3]) < 0.5

    def body(st):
        a, b, v, done = st
        p = a * 0.5 + b * 0.5
        stuck = (p <= a) | (p >= b)
        c = count(lambda s, j: s >= p)
        hit = c == kf
        is_done = done > 0.5
        v = jnp.where(is_done, v, jnp.where(stuck, a, jnp.where(hit, p, v)))
        frozen = is_done | stuck | hit
        a = jnp.where(frozen, a, jnp.where(c >= kf, p, a))
        b = jnp.where(frozen, b, jnp.where(c >= kf, b, p))
        return a, b, v, ones(frozen)

    _, _, v, _ = lax.while_loop(cond, body, (mn, mx, v0, done0))

    need = kf - count(lambda s, j: s > v)
    neq = count(lambda s, j: s == v)
    excess = (neq > need) & (v > NEG_INF)
    has_ties = jnp.max(ones(excess)) > 0.5
    key_index = lambda s, j: j * w + lax.broadcasted_iota(jnp.int32, s.shape, 1)

    @pl.when(has_ties)
    def _():
        nbits = (sc_ref.shape[0] * w).bit_length()
        cut = jnp.zeros(v.shape, jnp.int32)
        for bit in reversed(range(nbits)):
            cnd = cut + (1 << bit)
            cn = count(lambda s, j: (s == v) & (key_index(s, j) < cnd))
            cut = jnp.where(cn <= need, cnd, cut)

        def write(j):
            s = sc_ref[j]
            sel = ((s > v) | ((s == v) & (key_index(s, j) < cut))) & (s > NEG_INF)
            sc_ref[j] = jnp.where(sel, 0.0, NEG_INF)

        _for_blocks(nblk, write)

    @pl.when(jnp.logical_not(has_ties))
    def _():
        def write(j):
            s = sc_ref[j]
            sc_ref[j] = jnp.where((s >= v) & (s > NEG_INF), 0.0, NEG_INF)

        _for_blocks(nblk, write)


def _softmax_step(lg, vh, m, l, acc):
    m_new = jnp.maximum(m, jnp.max(lg, axis=-1, keepdims=True))
    alpha = jnp.exp(m - m_new)
    p = jnp.exp(lg - m_new)
    l = alpha * l + jnp.sum(p, axis=-1, keepdims=True)
    acc = alpha * acc + _dot1(p, vh)
    return m_new, l, acc


def _tile_bias(band_h, far_h, j, i):
    return jnp.where(j == i, band_h[:, TQ:], jnp.where(j == i - 1, band_h[:, :TQ], far_h))


def _dsa_prompt_kernel(qa_ref, qi_ref, sm_ref, kv_ref, ki_ref, band_ref, far_ref, o_ref,
                       sc_ref, qhi_ref, qlo_ref, *, topk):
    i = pl.program_id(1)
    row = lax.broadcasted_iota(jnp.int32, (TQ, TQ), 0)
    col = lax.broadcasted_iota(jnp.int32, (TQ, TQ), 1)
    wi = sm_ref[:, WI_LO:WI_HI]
    qs = _split(qi_ref[...])
    qhi_ref[...] = qs[0]
    qlo_ref[...] = qs[1]

    def score_body(j, carry):
        kid = _split(ki_ref[pl.ds(pl.multiple_of(j * TQ, TQ), TQ), :])
        sc = jnp.zeros((TQ, TQ), F32)
        for hh in range(IDX_HEADS):
            lanes = slice(hh * IDX_DIM, (hh + 1) * IDX_DIM)
            s = _dot3((qhi_ref[:, lanes], qlo_ref[:, lanes]), kid, NT) * (IDX_DIM ** -0.5)
            sc = sc + wi[:, hh:hh + 1] * jnp.maximum(s, 0.0)
        causal = (col + j * TQ) <= (row + i * TQ)
        sc_ref[j] = jnp.where(causal, sc, NEG_INF)
        return carry

    lax.fori_loop(0, i + 1, score_body, 0)
    _topk_to_mask(sc_ref, i + 1, topk)

    for h in range(N_HEADS):
        qh = (qa_ref[:, h * AB_DIM:(h + 1) * AB_DIM] * (AB_DIM ** -0.5)).astype(BF16)
        band_h = band_ref[h]
        far_h = far_ref[h][:, :1]

        def body(j, carry, h=h, qh=qh, band_h=band_h, far_h=far_h):
            rows = pl.ds(pl.multiple_of(j * TQ, TQ), TQ)
            kh = kv_ref[rows, h * AB_DIM:(h + 1) * AB_DIM]
            vh = kv_ref[rows, AB_WIDTH + h * AB_DIM:AB_WIDTH + (h + 1) * AB_DIM]
            lg = _dot1(qh, kh, NT) + _tile_bias(band_h, far_h, j, i) + sc_ref[j]
            return _softmax_step(lg, vh, *carry)

        init = (jnp.full((TQ, 1), M_INIT, F32), jnp.zeros((TQ, 1), F32), jnp.zeros((TQ, AB_DIM), F32))
        _, l, acc = lax.fori_loop(0, i + 1, body, init)
        o_ref[:, h * AB_DIM:(h + 1) * AB_DIM] = acc / l


def _dsa_prompt(qa, qi, small, kv_a, ki, band, far, batch, seq):
    nq = seq // TQ
    topk = min(DSA_TOPK, seq // 4)
    row_tile = lambda b, i: (b * nq + i, 0)
    whole = lambda b, i: (b, 0)
    return pl.pallas_call(
        functools.partial(_dsa_prompt_kernel, topk=topk),
        grid=(batch, nq),
        in_specs=[pl.BlockSpec((TQ, AB_WIDTH), row_tile),
                  pl.BlockSpec((TQ, AB_WIDTH), row_tile),
                  pl.BlockSpec((TQ, SMALL_W), row_tile),
                  pl.BlockSpec((seq, 2 * AB_WIDTH), whole),
                  pl.BlockSpec((seq, IDX_DIM), whole),
                  pl.BlockSpec(band.shape, lambda b, i: (0, 0, 0)),
                  pl.BlockSpec(far.shape, lambda b, i: (0, 0, 0))],
        out_specs=pl.BlockSpec((TQ, AB_WIDTH), row_tile),
        out_shape=jax.ShapeDtypeStruct((batch * seq, AB_WIDTH), F32),
        scratch_shapes=[pltpu.VMEM((nq, TQ, TQ), F32),
                        pltpu.VMEM((TQ, AB_WIDTH), BF16), pltpu.VMEM((TQ, AB_WIDTH), BF16)],
        compiler_params=pltpu.CompilerParams(
            dimension_semantics=("parallel", "arbitrary"), vmem_limit_bytes=VMEM_LIMIT),
        name="dsa_prompt",
    )(qa, qi, small, kv_a, ki, band, far)


def _softplus_parts(z):
    t = jnp.log1p(jnp.exp(-jnp.abs(z)))
    return -(jnp.maximum(-z, 0.0) + t), -(jnp.maximum(z, 0.0) + t)


def _sb_prompt_kernel(q_ref, kv_ref, u_ref, o_ref):
    i = pl.program_id(1)
    row = lax.broadcasted_iota(jnp.int32, (TQ, TQ), 0)
    col = lax.broadcasted_iota(jnp.int32, (TQ, TQ), 1)
    u = u_ref[...]
    for h in range(N_HEADS):
        qh = (q_ref[:, h * AB_DIM:(h + 1) * AB_DIM] * (AB_DIM ** -0.5)).astype(BF16)

        def body(t, carry, h=h, qh=qh):
            rs, acc = carry
            j = i - t
            rows = pl.ds(pl.multiple_of(j * TQ, TQ), TQ)
            kh = kv_ref[rows, h * AB_DIM:(h + 1) * AB_DIM]
            vh = kv_ref[rows, AB_WIDTH + h * AB_DIM:AB_WIDTH + (h + 1) * AB_DIM]
            z = _dot1(qh, kh, NT)
            mask = (col + j * TQ) < (row + i * TQ)
            log_beta, log_1m = _softplus_parts(z)
            log_1m = jnp.where(mask, log_1m, 0.0)
            suffix = _dot2(_split(log_1m), u) + rs
            w = jnp.where(mask, jnp.exp(log_beta + suffix), 0.0)
            return rs + jnp.sum(log_1m, axis=-1, keepdims=True), acc + _dot1(w, vh)

        init = (jnp.zeros((TQ, 1), F32), jnp.zeros((TQ, AB_DIM), F32))
        _, acc = lax.fori_loop(0, i + 1, body, init)
        o_ref[:, h * AB_DIM:(h + 1) * AB_DIM] = acc


def _sb_prompt(qb, kv_b, batch, seq):
    nq = seq // TQ
    row_tile = lambda b, i: (b * nq + i, 0)
    return pl.pallas_call(
        _sb_prompt_kernel,
        grid=(batch, nq),
        in_specs=[pl.BlockSpec((TQ, AB_WIDTH), row_tile),
                  pl.BlockSpec((seq, 2 * AB_WIDTH), lambda b, i: (b, 0)),
                  pl.BlockSpec((TQ, TQ), lambda b, i: (0, 0))],
        out_specs=pl.BlockSpec((TQ, AB_WIDTH), row_tile),
        out_shape=jax.ShapeDtypeStruct((batch * seq, AB_WIDTH), F32),
        compiler_params=pltpu.CompilerParams(
            dimension_semantics=("parallel", "arbitrary"), vmem_limit_bytes=VMEM_LIMIT),
        name="sb_prompt",
    )(qb, kv_b, _strict_lower_ones(TQ))


def _moba_select(gate, n_valid, axis):
    nb = gate.shape[axis]
    blk = lax.broadcasted_iota(jnp.int32, gate.shape, axis)
    gate = jnp.where(blk < n_valid, gate, NEG_INF)
    sel = jnp.zeros(gate.shape, F32)
    for n in range(nb):
        gn = gate[n:n + 1, :] if axis == 0 else gate[:, n:n + 1]
        beats = (gate > gn) | ((gate == gn) & (blk < n))
        rank = jnp.sum(jnp.where(beats, 1.0, 0.0), axis=axis, keepdims=True)
        chosen = (rank < float(MOBA_TOPK)) & (blk == n) & (blk < n_valid)
        sel = jnp.where(chosen, 1.0, sel)
    return sel


def _moba_prompt_kernel(q_ref, k_ref, v_ref, km_ref, band_ref, far_ref, o_ref):
    i = pl.program_id(2)
    row = lax.broadcasted_iota(jnp.int32, (TQ, TQ), 0)
    col = lax.broadcasted_iota(jnp.int32, (TQ, TQ), 1)
    q = q_ref[...]
    gate_t = _dot3(_split(km_ref[:, 0, :]), _split(q), NT)
    sel_t = _moba_select(gate_t, i, 0)
    sel = _pad_rows(sel_t, LANES).T
    blk = lax.broadcasted_iota(jnp.int32, sel.shape, 1)
    qs = (q * (C_DIM ** -0.5)).astype(BF16)
    band = band_ref[...]
    far = far_ref[...][:, :1]

    def body(j, carry):
        rows = pl.ds(pl.multiple_of(j * TQ, TQ), TQ)
        lg = _dot1(qs, k_ref[rows, :], NT) + _tile_bias(band, far, j, i)
        sel_j = jnp.sum(jnp.where(blk == j, sel, 0.0), axis=-1, keepdims=True)
        ok = jnp.where(j == i, jnp.where(col <= row, 1.0, 0.0), sel_j)
        lg = jnp.where(ok > 0.5, lg, NEG_INF)
        return _softmax_step(lg, v_ref[rows, :], *carry)

    init = (jnp.full((TQ, 1), M_INIT, F32), jnp.zeros((TQ, 1), F32), jnp.zeros((TQ, C_DIM), F32))
    _, l, acc = lax.fori_loop(0, i + 1, body, init)
    o_ref[...] = acc / l


def _moba_prompt(q, kv_c, kmean, band, far, batch, seq):
    nq = seq // TQ
    assert TQ == MOBA_BLOCK
    return pl.pallas_call(
        _moba_prompt_kernel,
        grid=(batch, N_HEADS, nq),
        in_specs=[pl.BlockSpec((TQ, C_DIM), lambda b, h, i: (b * nq + i, h)),
                  pl.BlockSpec((seq, C_DIM), lambda b, h, i: (b, h)),
                  pl.BlockSpec((seq, C_DIM), lambda b, h, i: (b, N_HEADS + h)),
                  pl.BlockSpec((nq, 1, C_DIM), lambda b, h, i: (b, 0, h)),
                  pl.BlockSpec((None, TQ, 2 * TQ), lambda b, h, i: (h, 0, 0)),
                  pl.BlockSpec((None, 1, LANES), lambda b, h, i: (h, 0, 0))],
        out_specs=pl.BlockSpec((TQ, C_DIM), lambda b, h, i: (b * nq + i, h)),
        out_shape=jax.ShapeDtypeStruct((batch * seq, C_WIDTH), F32),
        compiler_params=pltpu.CompilerParams(
            dimension_semantics=("parallel", "parallel", "arbitrary"), vmem_limit_bytes=VMEM_LIMIT),
        name="moba_prompt",
    )(q, kv_c, kv_c, kmean, band, far)


def _page_specs(n_pages, block, layer):
    zeros = (0,) * (len(block) - 2)
    return [pl.BlockSpec(block, lambda b, pt, p=p: (layer, pt[b, p]) + zeros) for p in range(n_pages)]


def _pad_rows(x, rows):
    return jnp.concatenate([x, jnp.zeros((rows - x.shape[0], x.shape[1]), x.dtype)], axis=0)


def _head_rows(q, hm):
    return jnp.concatenate([q] * N_HEADS, axis=0) * hm


def _head_diag(res, hm, t):
    out = res[0:t, :] * hm[0:t, :]
    for h in range(1, N_HEADS):
        out = out + res[h * t:(h + 1) * t, :] * hm[h * t:(h + 1) * t, :]
    return out


def _sample_spec(t_new, width):
    return pl.BlockSpec((t_new, width), lambda b, pt: (b, 0))


def _new_kv_t(kv_new, batch, t_new):
    w = kv_new.shape[1] // 2
    x = kv_new.reshape(batch, t_new, 2, w).transpose(0, 2, 3, 1)
    return jnp.pad(x, ((0, 0), (0, 0), (0, 0), (0, PAGE - t_new)))


def _dsa_sample_kernel(pt_ref, qa_ref, qir_ref, sm_ref, kvn_ref, kin_ref, hm_ref, bias_ref, *rest,
                       n_pages, t_new, topk):
    kv_pages = rest[:n_pages] + (kvn_ref,)
    ki_pages = rest[n_pages:2 * n_pages] + (kin_ref,)
    o_ref, sc_ref, lg_ref = rest[2 * n_pages:]
    wi = sm_ref[:, WI_LO:WI_HI]
    qir = _split(qir_ref[...])
    lane = lax.broadcasted_iota(jnp.int32, (t_new, PAGE), 1)
    trow = lax.broadcasted_iota(jnp.int32, (t_new, PAGE), 0)

    for p in range(n_pages + 1):
        r = jnp.maximum(_dot3(qir, _split(ki_pages[p][...])) * (IDX_DIM ** -0.5), 0.0)
        sc = jnp.zeros((t_new, PAGE), F32)
        for hh in range(IDX_HEADS):
            sc = sc + wi[:, hh:hh + 1] * r[hh * t_new:(hh + 1) * t_new, :]
        if p == n_pages:
            sc = jnp.where(lane <= trow, sc, NEG_INF)
        sc_ref[p] = sc

    _topk_to_mask(sc_ref, n_pages + 1, topk)

    hm = hm_ref[...]
    qrows = (_head_rows(qa_ref[...], hm) * (AB_DIM ** -0.5)).astype(BF16)
    m = jnp.full((N_HEADS * t_new, 1), M_INIT, F32)
    for p in range(n_pages + 1):
        mask = jnp.concatenate([sc_ref[p]] * N_HEADS, axis=0)
        lg = _dot1(qrows, kv_pages[p][0]) + bias_ref[p] + mask
        lg_ref[p] = lg
        m = jnp.maximum(m, jnp.max(lg, axis=-1, keepdims=True))
    l = jnp.zeros_like(m)
    acc = jnp.zeros((N_HEADS * t_new, AB_WIDTH), F32)
    for p in range(n_pages + 1):
        e = jnp.exp(lg_ref[p] - m)
        l = l + jnp.sum(e, axis=-1, keepdims=True)
        acc = acc + _dot1(e, kv_pages[p][1], NT)
    o_ref[...] = _head_diag(acc / l, hm, t_new)


def _dsa_sample(page_table, qa, qi_rows, small, kv_new_t, ki_new_t, hm, bias, cache_kv, cache_ki, layer, t_new):
    batch, n_pages = page_table.shape
    topk = min(DSA_TOPK, (n_pages * PAGE + t_new) // 4)
    in_specs = [_sample_spec(t_new, AB_WIDTH),
                pl.BlockSpec((None, IDX_HEADS * t_new, IDX_DIM), lambda b, pt: (b, 0, 0)),
                _sample_spec(t_new, SMALL_W),
                pl.BlockSpec((None, 2, AB_WIDTH, PAGE), lambda b, pt: (b, 0, 0, 0)),
                pl.BlockSpec((None, IDX_DIM, PAGE), lambda b, pt: (b, 0, 0)),
                pl.BlockSpec(hm.shape, lambda b, pt: (0, 0)),
                pl.BlockSpec(bias.shape, lambda b, pt: (0, 0, 0))]
    in_specs += _page_specs(n_pages, (None, None, 2, AB_WIDTH, PAGE), layer)
    in_specs += _page_specs(n_pages, (None, None, IDX_DIM, PAGE), layer)
    grid_spec = pltpu.PrefetchScalarGridSpec(
        num_scalar_prefetch=1, grid=(batch,), in_specs=in_specs,
        out_specs=_sample_spec(t_new, AB_WIDTH),
        scratch_shapes=[pltpu.VMEM((n_pages + 1, t_new, PAGE), F32),
                        pltpu.VMEM((n_pages + 1, N_HEADS * t_new, PAGE), F32)])
    return pl.pallas_call(
        functools.partial(_dsa_sample_kernel, n_pages=n_pages, t_new=t_new, topk=topk),
        grid_spec=grid_spec,
        out_shape=jax.ShapeDtypeStruct((batch * t_new, AB_WIDTH), F32),
        compiler_params=pltpu.CompilerParams(
            dimension_semantics=("arbitrary",), vmem_limit_bytes=VMEM_LIMIT),
        name="dsa_sample",
    )(page_table, qa, qi_rows, small, kv_new_t, ki_new_t, hm, bias,
      *([cache_kv] * n_pages), *([cache_ki] * n_pages))


def _sb_sample_kernel(pt_ref, q_ref, kvn_ref, hm_ref, u_ref, *rest, n_pages, t_new):
    kv_pages = rest[:n_pages] + (kvn_ref,)
    o_ref = rest[n_pages]
    hm = hm_ref[...]
    u = u_ref[...]
    qrows = (_head_rows(q_ref[...], hm) * (AB_DIM ** -0.5)).astype(BF16)
    shape = (N_HEADS * t_new, PAGE)
    lane = lax.broadcasted_iota(jnp.int32, shape, 1)
    trow = lax.rem(lax.broadcasted_iota(jnp.int32, shape, 0), t_new)
    rs = jnp.zeros((N_HEADS * t_new, 1), F32)
    acc = jnp.zeros((N_HEADS * t_new, AB_WIDTH), F32)
    for p in reversed(range(n_pages + 1)):
        z = _dot1(qrows, kv_pages[p][0])
        log_beta, log_1m = _softplus_parts(z)
        if p == n_pages:
            mask = lane < trow
            log_1m = jnp.where(mask, log_1m, 0.0)
        suffix = _dot2(_split(log_1m), u) + rs
        w = jnp.exp(log_beta + suffix)
        if p == n_pages:
            w = jnp.where(mask, w, 0.0)
        acc = acc + _dot1(w, kv_pages[p][1], NT)
        rs = rs + jnp.sum(log_1m, axis=-1, keepdims=True)
    o_ref[...] = _head_diag(acc, hm, t_new)


def _sb_sample(page_table, qb, kv_new_t, hm, cache_kv, layer, t_new):
    batch, n_pages = page_table.shape
    in_specs = [_sample_spec(t_new, AB_WIDTH),
                pl.BlockSpec((None, 2, AB_WIDTH, PAGE), lambda b, pt: (b, 0, 0, 0)),
                pl.BlockSpec(hm.shape, lambda b, pt: (0, 0)),
                pl.BlockSpec((PAGE, PAGE), lambda b, pt: (0, 0))]
    in_specs += _page_specs(n_pages, (None, None, 2, AB_WIDTH, PAGE), layer)
    grid_spec = pltpu.PrefetchScalarGridSpec(
        num_scalar_prefetch=1, grid=(batch,), in_specs=in_specs,
        out_specs=_sample_spec(t_new, AB_WIDTH))
    return pl.pallas_call(
        functools.partial(_sb_sample_kernel, n_pages=n_pages, t_new=t_new),
        grid_spec=grid_spec,
        out_shape=jax.ShapeDtypeStruct((batch * t_new, AB_WIDTH), F32),
        compiler_params=pltpu.CompilerParams(
            dimension_semantics=("arbitrary",), vmem_limit_bytes=VMEM_LIMIT),
        name="sb_sample",
    )(page_table, qb, kv_new_t, hm, _strict_lower_ones(PAGE), *([cache_kv] * n_pages))


def _moba_sample_kernel(pt_ref, q_ref, kvn_ref, bias_ref, *rest, n_pages, t_new):
    kv_pages = rest[:n_pages]
    o_ref, km_ref, lg_ref = rest[n_pages:]
    pages_per_block = MOBA_BLOCK // PAGE
    n_blocks = n_pages // pages_per_block
    rows_per_tok = 2 * N_HEADS
    for n in range(n_blocks):
        tot = jnp.zeros((rows_per_tok, C_DIM), F32)
        for r in range(pages_per_block):
            page = kv_pages[n * pages_per_block + r][...]
            tot = tot + jnp.sum(page.reshape(PAGE, rows_per_tok, C_DIM), axis=0)
        km_ref[n * N_HEADS:(n + 1) * N_HEADS, :] = tot[:N_HEADS, :] / MOBA_BLOCK
    lane = lax.broadcasted_iota(jnp.int32, (t_new, PAGE), 1)
    trow = lax.broadcasted_iota(jnp.int32, (t_new, PAGE), 0)
    kvn = _pad_rows(kvn_ref[...], PAGE)
    for h in range(N_HEADS):
        q = q_ref[:, h * C_DIM:(h + 1) * C_DIM]
        km_h = km_ref[pl.ds(h, n_blocks, stride=N_HEADS), :]
        gate = _dot3(_split(q), _split(km_h), NT)
        sel = _moba_select(gate, n_blocks, 1)
        qs = (q * (C_DIM ** -0.5)).astype(BF16)
        m = jnp.full((t_new, 1), M_INIT, F32)
        for p in range(n_pages + 1):
            if p < n_pages:
                k = kv_pages[p][pl.ds(h, PAGE, stride=rows_per_tok), :]
                n = p // pages_per_block
                ok = sel[:, n:n + 1] > 0.5
            else:
                k = kvn[:, h * C_DIM:(h + 1) * C_DIM]
                ok = lane <= trow
            lg = jnp.where(ok, _dot1(qs, k, NT) + bias_ref[p, h * t_new:(h + 1) * t_new, :], NEG_INF)
            lg_ref[p] = lg
            m = jnp.maximum(m, jnp.max(lg, axis=-1, keepdims=True))
        l = jnp.zeros_like(m)
        acc = jnp.zeros((t_new, C_DIM), F32)
        for p in range(n_pages + 1):
            if p < n_pages:
                v = kv_pages[p][pl.ds(N_HEADS + h, PAGE, stride=rows_per_tok), :]
            else:
                v = kvn[:, C_WIDTH + h * C_DIM:C_WIDTH + (h + 1) * C_DIM]
            e = jnp.exp(lg_ref[p] - m)
            l = l + jnp.sum(e, axis=-1, keepdims=True)
            acc = acc + _dot1(e, v)
        o_ref[:, h * C_DIM:(h + 1) * C_DIM] = acc / l


def _moba_sample(page_table, q, kv_new, bias, cache_kv, layer, t_new):
    batch, n_pages = page_table.shape
    assert (n_pages * PAGE) % MOBA_BLOCK == 0 and t_new <= MOBA_BLOCK
    in_specs = [_sample_spec(t_new, C_WIDTH),
                _sample_spec(t_new, 2 * C_WIDTH),
                pl.BlockSpec(bias.shape, lambda b, pt: (0, 0, 0))]
    in_specs += _page_specs(n_pages, (None, None, PAGE * 2 * N_HEADS, C_DIM), layer)
    grid_spec = pltpu.PrefetchScalarGridSpec(
        num_scalar_prefetch=1, grid=(batch,), in_specs=in_specs,
        out_specs=_sample_spec(t_new, C_WIDTH),
        scratch_shapes=[pltpu.VMEM((n_pages * PAGE // MOBA_BLOCK * N_HEADS, C_DIM), F32),
                        pltpu.VMEM((n_pages + 1, t_new, PAGE), F32)])
    return pl.pallas_call(
        functools.partial(_moba_sample_kernel, n_pages=n_pages, t_new=t_new),
        grid_spec=grid_spec,
        out_shape=jax.ShapeDtypeStruct((batch * t_new, C_WIDTH), F32),
        compiler_params=pltpu.CompilerParams(
            dimension_semantics=("arbitrary",), vmem_limit_bytes=VMEM_LIMIT),
        name="moba_sample",
    )(page_table, q, kv_new, bias, *([cache_kv] * n_pages))


def kernel(x_prompt, x_sample, cache_a_kv, cache_a_kidx, cache_b_kv, cache_c_kv, page_table, ln_w,
           w_in_ab, w_out_ab, qn_a, kn_a, w_in_c, w_out_c, qn_c, kn_c, rel_bias):
    batch, seq, _ = x_prompt.shape
    dec_batch, t_new, _ = x_sample.shape
    n_pages = page_table.shape[1]
    past_len = n_pages * PAGE
    depth = ln_w.shape[0]
    assert cache_a_kv.shape[2] == PAGE and seq % TQ == 0

    xp = x_prompt.reshape(batch * seq, D_MODEL)
    xs = x_sample.reshape(dec_batch * t_new, D_MODEL)
    n_pool = cache_a_kv.shape[1]
    ca_kv = jnp.transpose(cache_a_kv, (0, 1, 3, 4, 5, 2)).reshape(-1, n_pool, 2, AB_WIDTH, PAGE)
    cb_kv = jnp.transpose(cache_b_kv, (0, 1, 3, 4, 5, 2)).reshape(-1, n_pool, 2, AB_WIDTH, PAGE)
    ca_ki = jnp.transpose(cache_a_kidx, (0, 1, 3, 2))
    cc_kv = cache_c_kv.reshape(-1, n_pool, PAGE * 2 * N_HEADS, C_DIM)

    band, far = _prompt_bias(rel_bias)
    bias_s = _sample_bias(rel_bias, past_len, t_new)
    hm_ab = _head_mask(AB_WIDTH, t_new)
    mnorm_ab = _block_mean_matrix(GW, AB_DIM)
    mnorm_c = _block_mean_matrix(GW, C_DIM)

    outs = {k: [] for k in ("a_kv_p", "a_kv_s", "a_ki_p", "a_ki_s", "b_kv_p", "b_kv_s", "c_kv_p", "c_kv_s")}
    for l in range(depth):
        i = l // 2
        if l % 2 == 0:
            w_main, w_small, norm_w = _ab_weights(w_in_ab[i], qn_a[i], kn_a[i])
            w_out = _split(w_out_ab[i])
            proj = lambda x: _proj_call(x, ln_w[l], w_main, norm_w, mnorm_ab, w_small,
                                        _AB_PLAN, _AB_WIDTHS, False)
            kv_a, kv_b, g, qa, qi, qb, ki, small = proj(xp)
            oa = _dsa_prompt(qa, qi, small, kv_a, ki, band, far, batch, seq)
            ob = _sb_prompt(qb, kv_b, batch, seq)
            xp = _out_call(xp, g, [oa, ob], w_out)
            outs["a_kv_p"].append(kv_a)
            outs["a_ki_p"].append(ki)
            outs["b_kv_p"].append(kv_b)

            kv_a, kv_b, g, qa, qi, qb, ki, small = proj(xs)
            qi_rows = qi.reshape(dec_batch, t_new, IDX_HEADS, IDX_DIM).transpose(0, 2, 1, 3)
            qi_rows = qi_rows.reshape(dec_batch, IDX_HEADS * t_new, IDX_DIM)
            ki_t = jnp.pad(ki.reshape(dec_batch, t_new, IDX_DIM).transpose(0, 2, 1),
                           ((0, 0), (0, 0), (0, PAGE - t_new)))
            oa = _dsa_sample(page_table, qa, qi_rows, small, _new_kv_t(kv_a, dec_batch, t_new), ki_t,
                             hm_ab, bias_s, ca_kv, ca_ki, i, t_new)
            ob = _sb_sample(page_table, qb, _new_kv_t(kv_b, dec_batch, t_new), hm_ab, cb_kv, i, t_new)
            xs = _out_call(xs, g, [oa, ob], w_out)
            outs["a_kv_s"].append(kv_a)
            outs["a_ki_s"].append(ki)
            outs["b_kv_s"].append(kv_b)
        else:
            w_main, norm_w = _c_weights(w_in_c[i], qn_c[i], kn_c[i])
            w_out = _split(w_out_c[i])
            proj = lambda x, km: _proj_call(x, ln_w[l], w_main, norm_w, mnorm_c, None,
                                            _C_PLAN, _C_WIDTHS, km)
            kv_c, g, q, kmean = proj(xp, True)
            o = _moba_prompt(q, kv_c, kmean, band, far, batch, seq)
            xp = _out_call(xp, g, [o], w_out)
            outs["c_kv_p"].append(kv_c)

            kv_c, g, q = proj(xs, False)
            o = _moba_sample(page_table, q, kv_c, bias_s, cc_kv, i, t_new)
            xs = _out_call(xs, g, [o], w_out)
            outs["c_kv_s"].append(kv_c)

    def kv(name, b, t, d):
        return jnp.stack(outs[name]).reshape(-1, b, t, 2, N_HEADS, d)

    return (xp.reshape(batch, seq, D_MODEL), xs.reshape(dec_batch, t_new, D_MODEL),
            kv("a_kv_p", batch, seq, AB_DIM), kv("a_kv_s", dec_batch, t_new, AB_DIM),
            jnp.stack(outs["a_ki_p"]).reshape(-1, batch, seq, IDX_DIM),
            jnp.stack(outs["a_ki_s"]).reshape(-1, dec_batch, t_new, IDX_DIM),
            kv("b_kv_p", batch, seq, AB_DIM), kv("b_kv_s", dec_batch, t_new, AB_DIM),
            kv("c_kv_p", batch, seq, C_DIM), kv("c_kv_s", dec_batch, t_new, C_DIM))
```

```python
import functools
import math

import numpy as np
import jax
import jax.numpy as jnp
from jax import lax
from jax.experimental import pallas as pl
from jax.experimental.pallas import tpu as pltpu

F32 = jnp.float32
BF16 = jnp.bfloat16
HI = lax.Precision.HIGHEST
NEG_INF = float("-inf")
M_INIT = -1e30

D_MODEL = 1024
N_HEADS = 8
AB_DIM = 64
AB_WIDTH = N_HEADS * AB_DIM
C_DIM = 128
C_WIDTH = N_HEADS * C_DIM
IDX_HEADS = 8
IDX_DIM = 64
DSA_TOPK = 256
MOBA_BLOCK = 256
MOBA_TOPK = 3
PAGE = 128
REL_BUCKETS = 32
REL_MAX_DIST = 128
EPS = 1e-6

LANES = 128
TQ = MOBA_BLOCK
GW = 512
TM = 512
MOBA_HEADS_PER_STEP = 4
SMALL_W = LANES
WI_LO, WI_HI = IDX_DIM, IDX_DIM + IDX_HEADS
NN = (((1,), (0,)), ((), ()))
NT = (((1,), (1,)), ((), ()))
VMEM_LIMIT = 48 * 1024 * 1024


def _split(x):
    hi = x.astype(BF16)
    return hi, (x - hi.astype(F32)).astype(BF16)


def _dg(a, b, dims):
    return lax.dot_general(a, b, dims, preferred_element_type=F32)


def _dot3(a, b, dims=NN):
    return _dg(a[0], b[0], dims) + _dg(a[0], b[1], dims) + _dg(a[1], b[0], dims)


def _dot2(a, b_exact, dims=NN):
    return _dg(a[0], b_exact, dims) + _dg(a[1], b_exact, dims)


def _dot1(a, b, dims=NN):
    return _dg(a.astype(BF16), b.astype(BF16), dims)


def _t5_bucket_np(dist):
    dist = np.maximum(np.asarray(dist, np.int64), 0)
    max_exact = REL_BUCKETS // 2
    d = np.maximum(dist, max_exact).astype(np.float64)
    val = np.log(d / max_exact) / math.log(REL_MAX_DIST / max_exact) * (REL_BUCKETS - max_exact)
    large = np.minimum(max_exact + val.astype(np.int64), REL_BUCKETS - 1)
    return np.where(dist < max_exact, dist, large).astype(np.int32)


def _bias_lookup(rel_bias, buckets):
    onehot = (jnp.asarray(buckets.reshape(-1))[:, None] == jnp.arange(REL_BUCKETS)[None, :]).astype(F32)
    out = jnp.dot(onehot, rel_bias.astype(F32), precision=HI)
    return out.T.reshape((N_HEADS,) + buckets.shape)


def _prompt_bias(rel_bias):
    r = np.arange(TQ)[:, None]
    c = np.arange(2 * TQ)[None, :]
    band = _bias_lookup(rel_bias, _t5_bucket_np(TQ + r - c))
    far_bucket = _t5_bucket_np(np.array([TQ + 1, 1 << 20]))
    assert far_bucket[0] == far_bucket[1]
    far = jnp.broadcast_to(rel_bias[int(far_bucket[0])].astype(F32)[:, None, None], (N_HEADS, 1, LANES))
    return band, far


def _sample_bias(rel_bias, past_len, t_new):
    n_pages = past_len // PAGE
    p = np.arange(n_pages + 1)[:, None, None]
    t = np.arange(t_new)[None, :, None]
    c = np.arange(PAGE)[None, None, :]
    b = _bias_lookup(rel_bias, _t5_bucket_np(past_len + t - (p * PAGE + c)))
    return jnp.transpose(b, (1, 0, 2, 3)).reshape(n_pages + 1, N_HEADS * t_new, PAGE)


def _head_mask(width, rows_per_head):
    d = width // N_HEADS
    h_row = np.arange(N_HEADS * rows_per_head)[:, None] // rows_per_head
    h_col = np.arange(width)[None, :] // d
    return jnp.asarray((h_row == h_col).astype(np.float32))


def _block_mean_matrix(width, d):
    g = np.arange(width) // d
    return jnp.asarray((g[:, None] == g[None, :]).astype(np.float32) / d).astype(BF16)


def _strict_lower_ones(n):
    idx = np.arange(n)
    return jnp.asarray((idx[:, None] > idx[None, :]).astype(np.float32)).astype(BF16)


def _proj_kernel(*refs, plan, n_main_out, has_small, has_kmean):
    it = iter(refs)
    x_ref, lnw_ref, w_ref, nw_ref, m_ref = (next(it) for _ in range(5))
    ws_ref = next(it) if has_small else None
    outs = [next(it) for _ in range(n_main_out)]
    ki_ref = next(it) if has_small else None
    small_ref = next(it) if has_small else None
    km_ref = next(it) if has_kmean else None
    hb_ref = next(it)
    j = pl.program_id(1)

    @pl.when(j == 0)
    def _():
        x = x_ref[...]
        h = x * lax.rsqrt(jnp.mean(x * x, axis=-1, keepdims=True) + EPS) * lnw_ref[...]
        hb = h.astype(BF16)
        hb_ref[...] = hb
        if has_small:
            s = _dg(hb, ws_ref[...], NN)
            lane = lax.broadcasted_iota(jnp.int32, s.shape, 1)
            is_ki = lane < IDX_DIM
            ms = jnp.sum(jnp.where(is_ki, s * s, 0.0), axis=-1, keepdims=True) / IDX_DIM
            sm = jnp.where(is_ki, s * lax.rsqrt(ms + EPS), s * (IDX_HEADS ** -0.5))
            small_ref[...] = sm
            ki_ref[...] = sm[:, :IDX_DIM]

    acc = _dg(hb_ref[...], w_ref[...], NN)
    for jj, (oi, off, nrow, kmean_here) in enumerate(plan):
        @pl.when(j == jj)
        def _(oi=oi, off=off, nrow=nrow, kmean_here=kmean_here):
            val = acc
            if nrow is not None:
                ms = _dot2(_split(acc * acc), m_ref[...])
                val = acc * lax.rsqrt(ms + EPS) * nw_ref[nrow:nrow + 1, :]
            outs[oi][:, off:off + GW] = val
            if kmean_here and has_kmean:
                for r in range(val.shape[0] // MOBA_BLOCK):
                    blk = val[r * MOBA_BLOCK:(r + 1) * MOBA_BLOCK, :]
                    km_ref[r, :, off:off + GW] = jnp.sum(blk, axis=0, keepdims=True) / MOBA_BLOCK


def _proj_call(x, ln_w, w_main, norm_w, mnorm, w_small, plan, out_widths, has_kmean):
    n = x.shape[0]
    tm = min(TM, n)
    assert n % tm == 0 and w_main.shape[1] == GW * len(plan)
    has_small = w_small is not None
    assert not has_kmean or tm % MOBA_BLOCK == 0
    row = lambda i, j: (i, 0)
    const = lambda i, j: (0, 0)
    in_specs = [
        pl.BlockSpec((tm, D_MODEL), row),
        pl.BlockSpec((1, D_MODEL), const),
        pl.BlockSpec((D_MODEL, GW), lambda i, j: (0, j)),
        pl.BlockSpec(norm_w.shape, const),
        pl.BlockSpec((GW, GW), const),
    ]
    args = [x, ln_w.reshape(1, D_MODEL), w_main, norm_w, mnorm]
    if has_small:
        in_specs.append(pl.BlockSpec((D_MODEL, SMALL_W), const))
        args.append(w_small)
    out_shape = [jax.ShapeDtypeStruct((n, w), F32) for w in out_widths]
    out_specs = [pl.BlockSpec((tm, w), row) for w in out_widths]
    if has_small:
        out_shape += [jax.ShapeDtypeStruct((n, IDX_DIM), F32), jax.ShapeDtypeStruct((n, SMALL_W), F32)]
        out_specs += [pl.BlockSpec((tm, IDX_DIM), row), pl.BlockSpec((tm, SMALL_W), row)]
    if has_kmean:
        out_shape.append(jax.ShapeDtypeStruct((n // MOBA_BLOCK, 1, C_WIDTH), F32))
        out_specs.append(pl.BlockSpec((tm // MOBA_BLOCK, 1, C_WIDTH), lambda i, j: (i, 0, 0)))
    kern = functools.partial(_proj_kernel, plan=tuple(plan), n_main_out=len(out_widths),
                             has_small=has_small, has_kmean=has_kmean)
    return pl.pallas_call(
        kern,
        grid=(n // tm, len(plan)),
        in_specs=in_specs,
        out_specs=out_specs,
        out_shape=out_shape,
        scratch_shapes=[pltpu.VMEM((tm, D_MODEL), BF16)],
        compiler_params=pltpu.CompilerParams(
            dimension_semantics=("parallel", "arbitrary"), vmem_limit_bytes=VMEM_LIMIT),
        name="proj_ab" if has_small else "proj_c",
    )(*args)


def _ab_weights(w_in, qn, kn):
    offs = np.cumsum([0, 512, 512, 512, 512, 512, 64, 8, 512, 512, 512, 512])
    qa, ka, va, ga, qi, ki, wi, qb, kb, vb, gb = (w_in[:, offs[k]:offs[k + 1]] for k in range(11))
    w_main = jnp.concatenate([ka, va, kb, vb, ga, gb, qa, qi, qb], axis=1)
    w_small = jnp.concatenate([ki, wi, jnp.zeros((D_MODEL, SMALL_W - IDX_DIM - IDX_HEADS), F32)], axis=1)
    norm_w = jnp.stack([jnp.tile(kn, N_HEADS), jnp.tile(qn, N_HEADS)])
    return w_main.astype(BF16), w_small.astype(BF16), norm_w


_AB_PLAN = [(0, 0, 0, False), (0, 512, None, False),
            (1, 0, None, False), (1, 512, None, False),
            (2, 0, None, False), (2, 512, None, False),
            (3, 0, 1, False),
            (4, 0, None, False),
            (5, 0, None, False)]
_AB_WIDTHS = [1024, 1024, 1024, 512, 512, 512]

_C_PLAN = [(0, 0, 0, True), (0, 512, 0, True),
           (0, 1024, None, False), (0, 1536, None, False),
           (1, 0, None, False), (1, 512, None, False),
           (2, 0, 1, False), (2, 512, 1, False)]
_C_WIDTHS = [2048, 1024, 1024]


def _c_weights(w_in, qn, kn):
    q, k, v, g = (w_in[:, c * C_WIDTH:(c + 1) * C_WIDTH] for c in range(4))
    w_main = jnp.concatenate([k, v, g, q], axis=1)
    norm_w = jnp.stack([jnp.tile(kn, GW // C_DIM), jnp.tile(qn, GW // C_DIM)])
    return w_main.astype(BF16), norm_w


def _out_kernel(*refs, n_o):
    x_ref, g_ref = refs[0], refs[1]
    o_refs = refs[2:2 + n_o]
    w_ref, y_ref = refs[2 + n_o:]
    g = g_ref[...]
    sg = g * (1.0 / (1.0 + jnp.exp(-g)))
    y = x_ref[...]
    off = 0
    for o_ref in o_refs:
        w = o_ref.shape[1]
        y = y + _dot1(o_ref[...] * sg[:, off:off + w], w_ref[off:off + w, :])
        off += w
    y_ref[...] = y


def _out_call(x, g, os_, w_out):
    n = x.shape[0]
    tm = min(TM, n)
    assert n % tm == 0
    row = lambda i: (i, 0)
    in_specs = [pl.BlockSpec((tm, D_MODEL), row), pl.BlockSpec((tm, D_MODEL), row)]
    in_specs += [pl.BlockSpec((tm, o.shape[1]), row) for o in os_]
    in_specs.append(pl.BlockSpec((D_MODEL, D_MODEL), lambda i: (0, 0)))
    return pl.pallas_call(
        functools.partial(_out_kernel, n_o=len(os_)),
        grid=(n // tm,),
        in_specs=in_specs,
        out_specs=pl.BlockSpec((tm, D_MODEL), row),
        out_shape=jax.ShapeDtypeStruct((n, D_MODEL), F32),
        compiler_params=pltpu.CompilerParams(
            dimension_semantics=("parallel",), vmem_limit_bytes=VMEM_LIMIT),
        name="out_proj",
    )(x, g, *os_, w_out)


def _fold_blocks(sc_ref, nblk, f, comb, init_val):
    r, w = sc_ref.shape[1], sc_ref.shape[2]

    def lanes(x):
        out = x[:, :LANES]
        for c in range(1, w // LANES):
            out = comb(out, x[:, c * LANES:(c + 1) * LANES])
        return out

    init = jnp.full((r, LANES), init_val, F32)
    if isinstance(nblk, int):
        acc = init
        for j in range(nblk):
            acc = comb(acc, lanes(f(sc_ref[j], j)))
        return acc
    return lax.fori_loop(0, nblk, lambda j, acc: comb(acc, lanes(f(sc_ref[j], j))), init)


def _for_blocks(nblk, body):
    if isinstance(nblk, int):
        for j in range(nblk):
            body(j)
    else:
        lax.fori_loop(0, nblk, lambda j, c: (body(j), c)[1], 0)


def _topk_to_mask(sc_ref, nblk, k):
    kf = float(k)
    w = sc_ref.shape[2]
    ones = lambda m: jnp.where(m, 1.0, 0.0)
    add, fmin, fmax = jnp.add, jnp.minimum, jnp.maximum
    rsum = lambda acc: jnp.sum(acc, axis=-1, keepdims=True)
    count = lambda pred: rsum(_fold_blocks(sc_ref, nblk, lambda s, j: ones(pred(s, j)), add, 0.0))

    nvalid = count(lambda s, j: s > NEG_INF)
    mn = jnp.min(_fold_blocks(sc_ref, nblk, lambda s, j: jnp.where(s > NEG_INF, s, jnp.inf), fmin, jnp.inf),
                 axis=-1, keepdims=True)
    mx = jnp.max(_fold_blocks(sc_ref, nblk, lambda s, j: s, fmax, NEG_INF), axis=-1, keepdims=True)
    cmx = count(lambda s, j: s >= mx)
    cz_ge = count(lambda s, j: s >= 0.0)
    cz_gt = count(lambda s, j: s > 0.0)
    mpos = jnp.min(_fold_blocks(sc_ref, nblk, lambda s, j: jnp.where(s > 0.0, s, jnp.inf), fmin, jnp.inf),
                   axis=-1, keepdims=True)
    few = nvalid <= kf
    top = cmx >= kf
    zero = (cz_gt < kf) & (cz_ge >= kf)
    v0 = jnp.where(few, NEG_INF, jnp.where(top, mx, 0.0))
    done0 = ones(few | top | zero)
    mn = jnp.where(cz_gt >= kf, mpos, mn)
    mx = jnp.where(cz_ge < kf, 0.0, mx)

    def cond(st):
        return jnp.min(st[3]) < 0.5

    def body(st):
        a, b, v, done = st
        p = a * 0.5 + b * 0.5
        stuck = (p <= a) | (p >= b)
        c = count(lambda s, j: s >= p)
        hit = c == kf
        is_done = done > 0.5
        v = jnp.where(is_done, v, jnp.where(stuck, a, jnp.where(hit, p, v)))
        frozen = is_done | stuck | hit
        a = jnp.where(frozen, a, jnp.where(c >= kf, p, a))
        b = jnp.where(frozen, b, jnp.where(c >= kf, b, p))
        return a, b, v, ones(frozen)

    _, _, v, _ = lax.while_loop(cond, body, (mn, mx, v0, done0))

    need = kf - count(lambda s, j: s > v)
    neq = count(lambda s, j: s == v)
    excess = (neq > need) & (v > NEG_INF)
    has_ties = jnp.max(ones(excess)) > 0.5
    key_index = lambda s, j: j * w + lax.broadcasted_iota(jnp.int32, s.shape, 1)

    @pl.when(has_ties)
    def _():
        nbits = (sc_ref.shape[0] * w).bit_length()
        cut = jnp.zeros(v.shape, jnp.int32)
        for bit in reversed(range(nbits)):
            cnd = cut + (1 << bit)
            cn = count(lambda s, j: (s == v) & (key_index(s, j) < cnd))
            cut = jnp.where(cn <= need, cnd, cut)

        def write(j):
            s = sc_ref[j]
            sel = ((s > v) | ((s == v) & (key_index(s, j) < cut))) & (s > NEG_INF)
            sc_ref[j] = jnp.where(sel, 0.0, NEG_INF)

        _for_blocks(nblk, write)

    @pl.when(jnp.logical_not(has_ties))
    def _():
        def write(j):
            s = sc_ref[j]
            sc_ref[j] = jnp.where((s >= v) & (s > NEG_INF), 0.0, NEG_INF)

        _for_blocks(nblk, write)


def _flash_init(m_ref, l_ref, acc_ref):
    m_ref[...] = jnp.full(m_ref.shape, M_INIT, F32)
    l_ref[...] = jnp.zeros(l_ref.shape, F32)
    acc_ref[...] = jnp.zeros(acc_ref.shape, F32)


def _flash_update(h, lg, vh, m_ref, l_ref, acc_ref):
    m = m_ref[h]
    m_new = jnp.maximum(m, jnp.max(lg, axis=-1, keepdims=True))
    alpha = jnp.exp(m - m_new)
    p = jnp.exp(lg - m_new)
    m_ref[h] = m_new
    l_ref[h] = alpha * l_ref[h] + jnp.sum(p, axis=-1, keepdims=True)
    acc_ref[h] = alpha * acc_ref[h] + _dot1(p, vh)


def _for_key_tiles(i, step):
    lax.fori_loop(0, jnp.maximum(i - 1, 0), lambda j, c: (step(j, "far"), c)[1], 0)

    @pl.when(i >= 1)
    def _():
        step(i - 1, "prev")

    step(i, "diag")


def _tile_bias(band_ref, far_ref, h, kind):
    if kind == "far":
        return far_ref[h][:, :1]
    return band_ref[h, :, :TQ] if kind == "prev" else band_ref[h, :, TQ:]


def _dsa_prompt_kernel(qa_ref, qi_ref, sm_ref, kv_ref, ki_ref, band_ref, far_ref, o_ref,
                       sc_ref, qb_ref, m_ref, l_ref, acc_ref, *, topk):
    i = pl.program_id(1)
    row = lax.broadcasted_iota(jnp.int32, (TQ, TQ), 0)
    col = lax.broadcasted_iota(jnp.int32, (TQ, TQ), 1)
    wi = sm_ref[:, WI_LO:WI_HI]
    qb_ref[...] = qi_ref[...].astype(BF16)

    def score_body(j, carry):
        kid = ki_ref[pl.ds(pl.multiple_of(j * TQ, TQ), TQ), :].astype(BF16)
        sc = jnp.zeros((TQ, TQ), F32)
        for hh in range(IDX_HEADS):
            s = _dg(qb_ref[:, hh * IDX_DIM:(hh + 1) * IDX_DIM], kid, NT) * (IDX_DIM ** -0.5)
            sc = sc + wi[:, hh:hh + 1] * jnp.maximum(s, 0.0)
        causal = (col + j * TQ) <= (row + i * TQ)
        sc_ref[j] = jnp.where(causal, sc, NEG_INF)
        return carry

    lax.fori_loop(0, i + 1, score_body, 0)
    _topk_to_mask(sc_ref, i + 1, topk)

    qb_ref[...] = (qa_ref[...] * (AB_DIM ** -0.5)).astype(BF16)
    _flash_init(m_ref, l_ref, acc_ref)

    def step(j, kind):
        rows = pl.ds(pl.multiple_of(j * TQ, TQ), TQ)
        for h in range(N_HEADS):
            lanes = slice(h * AB_DIM, (h + 1) * AB_DIM)
            lg = _dot1(qb_ref[:, lanes], kv_ref[rows, lanes], NT)
            lg = lg + _tile_bias(band_ref, far_ref, h, kind) + sc_ref[j]
            vh = kv_ref[rows, AB_WIDTH + h * AB_DIM:AB_WIDTH + (h + 1) * AB_DIM]
            _flash_update(h, lg, vh, m_ref, l_ref, acc_ref)

    _for_key_tiles(i, step)
    for h in range(N_HEADS):
        o_ref[:, h * AB_DIM:(h + 1) * AB_DIM] = acc_ref[h] / l_ref[h]


def _dsa_prompt(qa, qi, small, kv_a, ki, band, far, batch, seq):
    nq = seq // TQ
    topk = min(DSA_TOPK, seq // 4)
    row_tile = lambda b, i: (b * nq + i, 0)
    whole = lambda b, i: (b, 0)
    return pl.pallas_call(
        functools.partial(_dsa_prompt_kernel, topk=topk),
        grid=(batch, nq),
        in_specs=[pl.BlockSpec((TQ, AB_WIDTH), row_tile),
                  pl.BlockSpec((TQ, AB_WIDTH), row_tile),
                  pl.BlockSpec((TQ, SMALL_W), row_tile),
                  pl.BlockSpec((seq, 2 * AB_WIDTH), whole),
                  pl.BlockSpec((seq, IDX_DIM), whole),
                  pl.BlockSpec(band.shape, lambda b, i: (0, 0, 0)),
                  pl.BlockSpec(far.shape, lambda b, i: (0, 0, 0))],
        out_specs=pl.BlockSpec((TQ, AB_WIDTH), row_tile),
        out_shape=jax.ShapeDtypeStruct((batch * seq, AB_WIDTH), F32),
        scratch_shapes=[pltpu.VMEM((nq, TQ, TQ), F32), pltpu.VMEM((TQ, AB_WIDTH), BF16),
                        pltpu.VMEM((N_HEADS, TQ, 1), F32), pltpu.VMEM((N_HEADS, TQ, 1), F32),
                        pltpu.VMEM((N_HEADS, TQ, AB_DIM), F32)],
        compiler_params=pltpu.CompilerParams(
            dimension_semantics=("parallel", "arbitrary"), vmem_limit_bytes=VMEM_LIMIT),
        name="dsa_prompt",
    )(qa, qi, small, kv_a, ki, band, far)


def _softplus_parts(z):
    t = jnp.log1p(jnp.exp(-jnp.abs(z)))
    return -(jnp.maximum(-z, 0.0) + t), -(jnp.maximum(z, 0.0) + t)


def _sb_prompt_kernel(q_ref, kv_ref, u_ref, o_ref, qb_ref, rs_ref, acc_ref):
    i = pl.program_id(1)
    row = lax.broadcasted_iota(jnp.int32, (TQ, TQ), 0)
    col = lax.broadcasted_iota(jnp.int32, (TQ, TQ), 1)
    qb_ref[...] = (q_ref[...] * (AB_DIM ** -0.5)).astype(BF16)
    rs_ref[...] = jnp.zeros(rs_ref.shape, F32)
    acc_ref[...] = jnp.zeros(acc_ref.shape, F32)

    def step(j, diag):
        rows = pl.ds(pl.multiple_of(j * TQ, TQ), TQ)
        for h in range(N_HEADS):
            lanes = slice(h * AB_DIM, (h + 1) * AB_DIM)
            z = _dot1(qb_ref[:, lanes], kv_ref[rows, lanes], NT)
            log_beta, log_1m = _softplus_parts(z)
            if diag:
                mask = col < row
                log_1m = jnp.where(mask, log_1m, 0.0)
            rs = rs_ref[h]
            suffix = _dot2(_split(log_1m), u_ref[...]) + rs
            w = jnp.exp(log_beta + suffix)
            if diag:
                w = jnp.where(mask, w, 0.0)
            vh = kv_ref[rows, AB_WIDTH + h * AB_DIM:AB_WIDTH + (h + 1) * AB_DIM]
            acc_ref[h] = acc_ref[h] + _dot1(w, vh)
            rs_ref[h] = rs + jnp.sum(log_1m, axis=-1, keepdims=True)

    step(i, True)
    lax.fori_loop(0, i, lambda t, c: (step(i - 1 - t, False), c)[1], 0)
    for h in range(N_HEADS):
        o_ref[:, h * AB_DIM:(h + 1) * AB_DIM] = acc_ref[h]


def _sb_prompt(qb, kv_b, batch, seq):
    nq = seq // TQ
    row_tile = lambda b, i: (b * nq + i, 0)
    return pl.pallas_call(
        _sb_prompt_kernel,
        grid=(batch, nq),
        in_specs=[pl.BlockSpec((TQ, AB_WIDTH), row_tile),
                  pl.BlockSpec((seq, 2 * AB_WIDTH), lambda b, i: (b, 0)),
                  pl.BlockSpec((TQ, TQ), lambda b, i: (0, 0))],
        out_specs=pl.BlockSpec((TQ, AB_WIDTH), row_tile),
        out_shape=jax.ShapeDtypeStruct((batch * seq, AB_WIDTH), F32),
        scratch_shapes=[pltpu.VMEM((TQ, AB_WIDTH), BF16), pltpu.VMEM((N_HEADS, TQ, 1), F32),
                        pltpu.VMEM((N_HEADS, TQ, AB_DIM), F32)],
        compiler_params=pltpu.CompilerParams(
            dimension_semantics=("parallel", "arbitrary"), vmem_limit_bytes=VMEM_LIMIT),
        name="sb_prompt",
    )(qb, kv_b, _strict_lower_ones(TQ))


def _moba_select(gate, n_valid, axis):
    nb = gate.shape[axis]
    blk = lax.broadcasted_iota(jnp.int32, gate.shape, axis)
    gate = jnp.where(blk < n_valid, gate, NEG_INF)
    sel = jnp.zeros(gate.shape, F32)
    for n in range(nb):
        gn = gate[n:n + 1, :] if axis == 0 else gate[:, n:n + 1]
        beats = (gate > gn) | ((gate == gn) & (blk < n))
        rank = jnp.sum(jnp.where(beats, 1.0, 0.0), axis=axis, keepdims=True)
        chosen = (rank < float(MOBA_TOPK)) & (blk == n) & (blk < n_valid)
        sel = jnp.where(chosen, 1.0, sel)
    return sel


def _moba_prompt_kernel(q_ref, k_ref, v_ref, km_ref, band_ref, far_ref, o_ref,
                        qb_ref, sel_ref, m_ref, l_ref, acc_ref):
    i = pl.program_id(2)
    hg = m_ref.shape[0]
    row = lax.broadcasted_iota(jnp.int32, (TQ, TQ), 0)
    col = lax.broadcasted_iota(jnp.int32, (TQ, TQ), 1)
    blk = lax.broadcasted_iota(jnp.int32, (TQ, LANES), 1)
    for h in range(hg):
        lanes = slice(h * C_DIM, (h + 1) * C_DIM)
        gate_t = _dot3(_split(km_ref[:, 0, lanes]), _split(q_ref[:, lanes]), NT)
        sel_t = _moba_select(gate_t, i, 0)
        sel_ref[h] = _pad_rows(sel_t, LANES).T
    qb_ref[...] = (q_ref[...] * (C_DIM ** -0.5)).astype(BF16)
    _flash_init(m_ref, l_ref, acc_ref)

    def step(j, kind):
        rows = pl.ds(pl.multiple_of(j * TQ, TQ), TQ)
        for h in range(hg):
            lanes = slice(h * C_DIM, (h + 1) * C_DIM)
            lg = _dot1(qb_ref[:, lanes], k_ref[rows, lanes], NT) + _tile_bias(band_ref, far_ref, h, kind)
            if kind == "diag":
                ok = col <= row
            else:
                ok = jnp.sum(jnp.where(blk == j, sel_ref[h], 0.0), axis=-1, keepdims=True) > 0.5
            lg = jnp.where(ok, lg, NEG_INF)
            _flash_update(h, lg, v_ref[rows, lanes], m_ref, l_ref, acc_ref)

    _for_key_tiles(i, step)
    for h in range(hg):
        o_ref[:, h * C_DIM:(h + 1) * C_DIM] = acc_ref[h] / l_ref[h]


def _moba_prompt(q, kv_c, kmean, band, far, batch, seq):
    nq = seq // TQ
    hg = MOBA_HEADS_PER_STEP
    gw = hg * C_DIM
    assert TQ == MOBA_BLOCK and N_HEADS % hg == 0
    q_tile = lambda b, g, i: (b * nq + i, g)
    return pl.pallas_call(
        _moba_prompt_kernel,
        grid=(batch, N_HEADS // hg, nq),
        in_specs=[pl.BlockSpec((TQ, gw), q_tile),
                  pl.BlockSpec((seq, gw), lambda b, g, i: (b, g)),
                  pl.BlockSpec((seq, gw), lambda b, g, i: (b, N_HEADS // hg + g)),
                  pl.BlockSpec((nq, 1, gw), lambda b, g, i: (b, 0, g)),
                  pl.BlockSpec((hg, TQ, 2 * TQ), lambda b, g, i: (g, 0, 0)),
                  pl.BlockSpec((hg, 1, LANES), lambda b, g, i: (g, 0, 0))],
        out_specs=pl.BlockSpec((TQ, gw), q_tile),
        out_shape=jax.ShapeDtypeStruct((batch * seq, C_WIDTH), F32),
        scratch_shapes=[pltpu.VMEM((TQ, gw), BF16), pltpu.VMEM((hg, TQ, LANES), F32),
                        pltpu.VMEM((hg, TQ, 1), F32), pltpu.VMEM((hg, TQ, 1), F32),
                        pltpu.VMEM((hg, TQ, C_DIM), F32)],
        compiler_params=pltpu.CompilerParams(
            dimension_semantics=("parallel", "parallel", "arbitrary"), vmem_limit_bytes=VMEM_LIMIT),
        name="moba_prompt",
    )(q, kv_c, kv_c, kmean, band, far)


def _page_specs(n_pages, block, layer):
    zeros = (0,) * (len(block) - 2)
    return [pl.BlockSpec(block, lambda b, pt, p=p: (layer, pt[b, p]) + zeros) for p in range(n_pages)]


def _pad_rows(x, rows):
    return jnp.concatenate([x, jnp.zeros((rows - x.shape[0], x.shape[1]), x.dtype)], axis=0)


def _head_rows(q, hm):
    return jnp.concatenate([q] * N_HEADS, axis=0) * hm


def _head_diag(res, hm, t):
    out = res[0:t, :] * hm[0:t, :]
    for h in range(1, N_HEADS):
        out = out + res[h * t:(h + 1) * t, :] * hm[h * t:(h + 1) * t, :]
    return out


def _sample_spec(t_new, width):
    return pl.BlockSpec((t_new, width), lambda b, pt: (b, 0))


def _new_kv_t(kv_new, batch, t_new):
    w = kv_new.shape[1] // 2
    x = kv_new.reshape(batch, t_new, 2, w).transpose(0, 2, 3, 1)
    return jnp.pad(x, ((0, 0), (0, 0), (0, 0), (0, PAGE - t_new)))


def _dsa_sample_kernel(pt_ref, qa_ref, qir_ref, sm_ref, kvn_ref, kin_ref, hm_ref, bias_ref, *rest,
                       n_pages, t_new, topk):
    kv_pages = rest[:n_pages] + (kvn_ref,)
    ki_pages = rest[n_pages:2 * n_pages] + (kin_ref,)
    o_ref, sc_ref, lg_ref = rest[2 * n_pages:]
    wi = sm_ref[:, WI_LO:WI_HI]
    qir = _split(qir_ref[...])
    lane = lax.broadcasted_iota(jnp.int32, (t_new, PAGE), 1)
    trow = lax.broadcasted_iota(jnp.int32, (t_new, PAGE), 0)

    for p in range(n_pages + 1):
        r = jnp.maximum(_dot3(qir, _split(ki_pages[p][...])) * (IDX_DIM ** -0.5), 0.0)
        sc = jnp.zeros((t_new, PAGE), F32)
        for hh in range(IDX_HEADS):
            sc = sc + wi[:, hh:hh + 1] * r[hh * t_new:(hh + 1) * t_new, :]
        if p == n_pages:
            sc = jnp.where(lane <= trow, sc, NEG_INF)
        sc_ref[p] = sc

    _topk_to_mask(sc_ref, n_pages + 1, topk)

    hm = hm_ref[...]
    qrows = (_head_rows(qa_ref[...], hm) * (AB_DIM ** -0.5)).astype(BF16)
    m = jnp.full((N_HEADS * t_new, 1), M_INIT, F32)
    for p in range(n_pages + 1):
        mask = jnp.concatenate([sc_ref[p]] * N_HEADS, axis=0)
        lg = _dot1(qrows, kv_pages[p][0]) + bias_ref[p] + mask
        lg_ref[p] = lg
        m = jnp.maximum(m, jnp.max(lg, axis=-1, keepdims=True))
    l = jnp.zeros_like(m)
    acc = jnp.zeros((N_HEADS * t_new, AB_WIDTH), F32)
    for p in range(n_pages + 1):
        e = jnp.exp(lg_ref[p] - m)
        l = l + jnp.sum(e, axis=-1, keepdims=True)
        acc = acc + _dot1(e, kv_pages[p][1], NT)
    o_ref[...] = _head_diag(acc / l, hm, t_new)


def _dsa_sample(page_table, qa, qi_rows, small, kv_new_t, ki_new_t, hm, bias, cache_kv, cache_ki, layer, t_new):
    batch, n_pages = page_table.shape
    topk = min(DSA_TOPK, (n_pages * PAGE + t_new) // 4)
    in_specs = [_sample_spec(t_new, AB_WIDTH),
                pl.BlockSpec((None, IDX_HEADS * t_new, IDX_DIM), lambda b, pt: (b, 0, 0)),
                _sample_spec(t_new, SMALL_W),
                pl.BlockSpec((None, 2, AB_WIDTH, PAGE), lambda b, pt: (b, 0, 0, 0)),
                pl.BlockSpec((None, IDX_DIM, PAGE), lambda b, pt: (b, 0, 0)),
                pl.BlockSpec(hm.shape, lambda b, pt: (0, 0)),
                pl.BlockSpec(bias.shape, lambda b, pt: (0, 0, 0))]
    in_specs += _page_specs(n_pages, (None, None, 2, AB_WIDTH, PAGE), layer)
    in_specs += _page_specs(n_pages, (None, None, IDX_DIM, PAGE), layer)
    grid_spec = pltpu.PrefetchScalarGridSpec(
        num_scalar_prefetch=1, grid=(batch,), in_specs=in_specs,
        out_specs=_sample_spec(t_new, AB_WIDTH),
        scratch_shapes=[pltpu.VMEM((n_pages + 1, t_new, PAGE), F32),
                        pltpu.VMEM((n_pages + 1, N_HEADS * t_new, PAGE), F32)])
    return pl.pallas_call(
        functools.partial(_dsa_sample_kernel, n_pages=n_pages, t_new=t_new, topk=topk),
        grid_spec=grid_spec,
        out_shape=jax.ShapeDtypeStruct((batch * t_new, AB_WIDTH), F32),
        compiler_params=pltpu.CompilerParams(
            dimension_semantics=("arbitrary",), vmem_limit_bytes=VMEM_LIMIT),
        name="dsa_sample",
    )(page_table, qa, qi_rows, small, kv_new_t, ki_new_t, hm, bias,
      *([cache_kv] * n_pages), *([cache_ki] * n_pages))


def _sb_sample_kernel(pt_ref, q_ref, kvn_ref, hm_ref, u_ref, *rest, n_pages, t_new):
    kv_pages = rest[:n_pages] + (kvn_ref,)
    o_ref = rest[n_pages]
    hm = hm_ref[...]
    u = u_ref[...]
    qrows = (_head_rows(q_ref[...], hm) * (AB_DIM ** -0.5)).astype(BF16)
    shape = (N_HEADS * t_new, PAGE)
    lane = lax.broadcasted_iota(jnp.int32, shape, 1)
    trow = lax.rem(lax.broadcasted_iota(jnp.int32, shape, 0), t_new)
    rs = jnp.zeros((N_HEADS * t_new, 1), F32)
    acc = jnp.zeros((N_HEADS * t_new, AB_WIDTH), F32)
    for p in reversed(range(n_pages + 1)):
        z = _dot1(qrows, kv_pages[p][0])
        log_beta, log_1m = _softplus_parts(z)
        if p == n_pages:
            mask = lane < trow
            log_1m = jnp.where(mask, log_1m, 0.0)
        suffix = _dot2(_split(log_1m), u) + rs
        w = jnp.exp(log_beta + suffix)
        if p == n_pages:
            w = jnp.where(mask, w, 0.0)
        acc = acc + _dot1(w, kv_pages[p][1], NT)
        rs = rs + jnp.sum(log_1m, axis=-1, keepdims=True)
    o_ref[...] = _head_diag(acc, hm, t_new)


def _sb_sample(page_table, qb, kv_new_t, hm, cache_kv, layer, t_new):
    batch, n_pages = page_table.shape
    in_specs = [_sample_spec(t_new, AB_WIDTH),
                pl.BlockSpec((None, 2, AB_WIDTH, PAGE), lambda b, pt: (b, 0, 0, 0)),
                pl.BlockSpec(hm.shape, lambda b, pt: (0, 0)),
                pl.BlockSpec((PAGE, PAGE), lambda b, pt: (0, 0))]
    in_specs += _page_specs(n_pages, (None, None, 2, AB_WIDTH, PAGE), layer)
    grid_spec = pltpu.PrefetchScalarGridSpec(
        num_scalar_prefetch=1, grid=(batch,), in_specs=in_specs,
        out_specs=_sample_spec(t_new, AB_WIDTH))
    return pl.pallas_call(
        functools.partial(_sb_sample_kernel, n_pages=n_pages, t_new=t_new),
        grid_spec=grid_spec,
        out_shape=jax.ShapeDtypeStruct((batch * t_new, AB_WIDTH), F32),
        compiler_params=pltpu.CompilerParams(
            dimension_semantics=("arbitrary",), vmem_limit_bytes=VMEM_LIMIT),
        name="sb_sample",
    )(page_table, qb, kv_new_t, hm, _strict_lower_ones(PAGE), *([cache_kv] * n_pages))


def _moba_sample_kernel(pt_ref, q_ref, kvn_ref, bias_ref, *rest, n_pages, t_new):
    kv_pages = rest[:n_pages]
    o_ref, km_ref, lg_ref = rest[n_pages:]
    pages_per_block = MOBA_BLOCK // PAGE
    n_blocks = n_pages // pages_per_block
    rows_per_tok = 2 * N_HEADS
    for n in range(n_blocks):
        tot = jnp.zeros((rows_per_tok, C_DIM), F32)
        for r in range(pages_per_block):
            page = kv_pages[n * pages_per_block + r][...]
            tot = tot + jnp.sum(page.reshape(PAGE, rows_per_tok, C_DIM), axis=0)
        km_ref[n * N_HEADS:(n + 1) * N_HEADS, :] = tot[:N_HEADS, :] / MOBA_BLOCK
    lane = lax.broadcasted_iota(jnp.int32, (t_new, PAGE), 1)
    trow = lax.broadcasted_iota(jnp.int32, (t_new, PAGE), 0)
    kvn = _pad_rows(kvn_ref[...], PAGE)
    for h in range(N_HEADS):
        q = q_ref[:, h * C_DIM:(h + 1) * C_DIM]
        km_h = km_ref[pl.ds(h, n_blocks, stride=N_HEADS), :]
        gate = _dot3(_split(q), _split(km_h), NT)
        sel = _moba_select(gate, n_blocks, 1)
        qs = (q * (C_DIM ** -0.5)).astype(BF16)
        m = jnp.full((t_new, 1), M_INIT, F32)
        for p in range(n_pages + 1):
            if p < n_pages:
                k = kv_pages[p][pl.ds(h, PAGE, stride=rows_per_tok), :]
                n = p // pages_per_block
                ok = sel[:, n:n + 1] > 0.5
            else:
                k = kvn[:, h * C_DIM:(h + 1) * C_DIM]
                ok = lane <= trow
            lg = jnp.where(ok, _dot1(qs, k, NT) + bias_ref[p, h * t_new:(h + 1) * t_new, :], NEG_INF)
            lg_ref[p] = lg
            m = jnp.maximum(m, jnp.max(lg, axis=-1, keepdims=True))
        l = jnp.zeros_like(m)
        acc = jnp.zeros((t_new, C_DIM), F32)
        for p in range(n_pages + 1):
            if p < n_pages:
                v = kv_pages[p][pl.ds(N_HEADS + h, PAGE, stride=rows_per_tok), :]
            else:
                v = kvn[:, C_WIDTH + h * C_DIM:C_WIDTH + (h + 1) * C_DIM]
            e = jnp.exp(lg_ref[p] - m)
            l = l + jnp.sum(e, axis=-1, keepdims=True)
            acc = acc + _dot1(e, v)
        o_ref[:, h * C_DIM:(h + 1) * C_DIM] = acc / l


def _moba_sample(page_table, q, kv_new, bias, cache_kv, layer, t_new):
    batch, n_pages = page_table.shape
    assert (n_pages * PAGE) % MOBA_BLOCK == 0 and t_new <= MOBA_BLOCK
    in_specs = [_sample_spec(t_new, C_WIDTH),
                _sample_spec(t_new, 2 * C_WIDTH),
                pl.BlockSpec(bias.shape, lambda b, pt: (0, 0, 0))]
    in_specs += _page_specs(n_pages, (None, None, PAGE * 2 * N_HEADS, C_DIM), layer)
    grid_spec = pltpu.PrefetchScalarGridSpec(
        num_scalar_prefetch=1, grid=(batch,), in_specs=in_specs,
        out_specs=_sample_spec(t_new, C_WIDTH),
        scratch_shapes=[pltpu.VMEM((n_pages * PAGE // MOBA_BLOCK * N_HEADS, C_DIM), F32),
                        pltpu.VMEM((n_pages + 1, t_new, PAGE), F32)])
    return pl.pallas_call(
        functools.partial(_moba_sample_kernel, n_pages=n_pages, t_new=t_new),
        grid_spec=grid_spec,
        out_shape=jax.ShapeDtypeStruct((batch * t_new, C_WIDTH), F32),
        compiler_params=pltpu.CompilerParams(
            dimension_semantics=("arbitrary",), vmem_limit_bytes=VMEM_LIMIT),
        name="moba_sample",
    )(page_table, q, kv_new, bias, *([cache_kv] * n_pages))


def kernel(x_prompt, x_sample, cache_a_kv, cache_a_kidx, cache_b_kv, cache_c_kv, page_table, ln_w,
           w_in_ab, w_out_ab, qn_a, kn_a, w_in_c, w_out_c, qn_c, kn_c, rel_bias):
    batch, seq, _ = x_prompt.shape
    dec_batch, t_new, _ = x_sample.shape
    n_pages = page_table.shape[1]
    past_len = n_pages * PAGE
    depth = ln_w.shape[0]
    assert cache_a_kv.shape[2] == PAGE and seq % TQ == 0

    xp = x_prompt.reshape(batch * seq, D_MODEL)
    xs = x_sample.reshape(dec_batch * t_new, D_MODEL)
    n_pool = cache_a_kv.shape[1]
    ca_kv = jnp.transpose(cache_a_kv, (0, 1, 3, 4, 5, 2)).reshape(-1, n_pool, 2, AB_WIDTH, PAGE)
    cb_kv = jnp.transpose(cache_b_kv, (0, 1, 3, 4, 5, 2)).reshape(-1, n_pool, 2, AB_WIDTH, PAGE)
    ca_ki = jnp.transpose(cache_a_kidx, (0, 1, 3, 2))
    cc_kv = cache_c_kv.reshape(-1, n_pool, PAGE * 2 * N_HEADS, C_DIM)

    band, far = _prompt_bias(rel_bias)
    bias_s = _sample_bias(rel_bias, past_len, t_new)
    hm_ab = _head_mask(AB_WIDTH, t_new)
    mnorm_ab = _block_mean_matrix(GW, AB_DIM)
    mnorm_c = _block_mean_matrix(GW, C_DIM)

    outs = {k: [] for k in ("a_kv_p", "a_kv_s", "a_ki_p", "a_ki_s", "b_kv_p", "b_kv_s", "c_kv_p", "c_kv_s")}
    for l in range(depth):
        i = l // 2
        if l % 2 == 0:
            w_main, w_small, norm_w = _ab_weights(w_in_ab[i], qn_a[i], kn_a[i])
            w_out = w_out_ab[i].astype(BF16)
            proj = lambda x: _proj_call(x, ln_w[l], w_main, norm_w, mnorm_ab, w_small,
                                        _AB_PLAN, _AB_WIDTHS, False)
            kv_a, kv_b, g, qa, qi, qb, ki, small = proj(xp)
            oa = _dsa_prompt(qa, qi, small, kv_a, ki, band, far, batch, seq)
            ob = _sb_prompt(qb, kv_b, batch, seq)
            xp = _out_call(xp, g, [oa, ob], w_out)
            outs["a_kv_p"].append(kv_a)
            outs["a_ki_p"].append(ki)
            outs["b_kv_p"].append(kv_b)

            kv_a, kv_b, g, qa, qi, qb, ki, small = proj(xs)
            qi_rows = qi.reshape(dec_batch, t_new, IDX_HEADS, IDX_DIM).transpose(0, 2, 1, 3)
            qi_rows = qi_rows.reshape(dec_batch, IDX_HEADS * t_new, IDX_DIM)
            ki_t = jnp.pad(ki.reshape(dec_batch, t_new, IDX_DIM).transpose(0, 2, 1),
                           ((0, 0), (0, 0), (0, PAGE - t_new)))
            oa = _dsa_sample(page_table, qa, qi_rows, small, _new_kv_t(kv_a, dec_batch, t_new), ki_t,
                             hm_ab, bias_s, ca_kv, ca_ki, i, t_new)
            ob = _sb_sample(page_table, qb, _new_kv_t(kv_b, dec_batch, t_new), hm_ab, cb_kv, i, t_new)
            xs = _out_call(xs, g, [oa, ob], w_out)
            outs["a_kv_s"].append(kv_a)
            outs["a_ki_s"].append(ki)
            outs["b_kv_s"].append(kv_b)
        else:
            w_main, norm_w = _c_weights(w_in_c[i], qn_c[i], kn_c[i])
            w_out = w_out_c[i].astype(BF16)
            proj = lambda x, km: _proj_call(x, ln_w[l], w_main, norm_w, mnorm_c, None,
                                            _C_PLAN, _C_WIDTHS, km)
            kv_c, g, q, kmean = proj(xp, True)
            o = _moba_prompt(q, kv_c, kmean, band, far, batch, seq)
            xp = _out_call(xp, g, [o], w_out)
            outs["c_kv_p"].append(kv_c)

            kv_c, g, q = proj(xs, False)
            o = _moba_sample(page_table, q, kv_c, bias_s, cc_kv, i, t_new)
            xs = _out_call(xs, g, [o], w_out)
            outs["c_kv_s"].append(kv_c)

    def kv(name, b, t, d):
        return jnp.stack(outs[name]).reshape(-1, b, t, 2, N_HEADS, d)

    return (xp.reshape(batch, seq, D_MODEL), xs.reshape(dec_batch, t_new, D_MODEL),
            kv("a_kv_p", batch, seq, AB_DIM), kv("a_kv_s", dec_batch, t_new, AB_DIM),
            jnp.stack(outs["a_ki_p"]).reshape(-1, batch, seq, IDX_DIM),
            jnp.stack(outs["a_ki_s"]).reshape(-1, dec_batch, t_new, IDX_DIM),
            kv("b_kv_p", batch, seq, AB_DIM), kv("b_kv_s", dec_batch, t_new, AB_DIM),
            kv("c_kv_p", batch, seq, C_DIM), kv("c_kv_s", dec_batch, t_new, C_DIM))
```

```python
import functools
import math

import numpy as np
import jax
import jax.numpy as jnp
from jax import lax
from jax.experimental import pallas as pl
from jax.experimental.pallas import tpu as pltpu

F32 = jnp.float32
BF16 = jnp.bfloat16
HI = lax.Precision.HIGHEST
NEG_INF = float("-inf")
M_INIT = -1e30

D_MODEL = 1024
N_HEADS = 8
AB_DIM = 64
AB_WIDTH = N_HEADS * AB_DIM
C_DIM = 128
C_WIDTH = N_HEADS * C_DIM
IDX_HEADS = 8
IDX_DIM = 64
DSA_TOPK = 256
MOBA_BLOCK = 256
MOBA_TOPK = 3
PAGE = 128
REL_BUCKETS = 32
REL_MAX_DIST = 128
EPS = 1e-6

LANES = 128
SUBLANES = 8
TQ = MOBA_BLOCK
GW = 512
TM = 512
MOBA_HEADS_PER_STEP = 4
SMALL_W = LANES
WI_LO, WI_HI = IDX_DIM, IDX_DIM + IDX_HEADS
NN = (((1,), (0,)), ((), ()))
NT = (((1,), (1,)), ((), ()))
TN = (((0,), (0,)), ((), ()))
VMEM_LIMIT = 48 * 1024 * 1024


def _split(x):
    hi = x.astype(BF16)
    return hi, (x - hi.astype(F32)).astype(BF16)


def _dg(a, b, dims):
    return lax.dot_general(a, b, dims, preferred_element_type=F32)


def _dot3(a, b, dims=NN):
    return _dg(a[0], b[0], dims) + _dg(a[0], b[1], dims) + _dg(a[1], b[0], dims)


def _dot2(a, b_exact, dims=NN):
    return _dg(a[0], b_exact, dims) + _dg(a[1], b_exact, dims)


def _dot1(a, b, dims=NN):
    return _dg(a.astype(BF16), b.astype(BF16), dims)


def _t5_bucket_np(dist):
    dist = np.maximum(np.asarray(dist, np.int64), 0)
    max_exact = REL_BUCKETS // 2
    d = np.maximum(dist, max_exact).astype(np.float64)
    val = np.log(d / max_exact) / math.log(REL_MAX_DIST / max_exact) * (REL_BUCKETS - max_exact)
    large = np.minimum(max_exact + val.astype(np.int64), REL_BUCKETS - 1)
    return np.where(dist < max_exact, dist, large).astype(np.int32)


def _bias_lookup(rel_bias, buckets):
    onehot = (jnp.asarray(buckets.reshape(-1))[:, None] == jnp.arange(REL_BUCKETS)[None, :]).astype(F32)
    out = jnp.dot(onehot, rel_bias.astype(F32), precision=HI)
    return out.T.reshape((N_HEADS,) + buckets.shape)


def _prompt_bias(rel_bias):
    c = np.arange(2 * TQ)[:, None]
    r = np.arange(TQ)[None, :]
    band = _bias_lookup(rel_bias, _t5_bucket_np(TQ + r - c))
    far_bucket = _t5_bucket_np(np.array([TQ + 1, 1 << 20]))
    assert far_bucket[0] == far_bucket[1]
    far = jnp.broadcast_to(rel_bias[int(far_bucket[0])].astype(F32)[:, None, None], (N_HEADS, 1, LANES))
    return band, far


def _sample_bias(rel_bias, past_len, t_new):
    n_pages = past_len // PAGE
    p = np.arange(n_pages + 1)[:, None, None]
    t = np.arange(t_new)[None, :, None]
    c = np.arange(PAGE)[None, None, :]
    b = _bias_lookup(rel_bias, _t5_bucket_np(past_len + t - (p * PAGE + c)))
    return jnp.transpose(b, (1, 0, 2, 3)).reshape(n_pages + 1, N_HEADS * t_new, PAGE)


def _head_mask(width, rows_per_head):
    d = width // N_HEADS
    h_row = np.arange(N_HEADS * rows_per_head)[:, None] // rows_per_head
    h_col = np.arange(width)[None, :] // d
    return jnp.asarray((h_row == h_col).astype(np.float32))


def _block_mean_matrix(width, d):
    g = np.arange(width) // d
    return jnp.asarray((g[:, None] == g[None, :]).astype(np.float32) / d).astype(BF16)


def _strict_lower_ones(n):
    idx = np.arange(n)
    return jnp.asarray((idx[:, None] > idx[None, :]).astype(np.float32)).astype(BF16)


def _proj_kernel(*refs, plan, n_main_out, has_small, has_kmean):
    it = iter(refs)
    x_ref, lnw_ref, w_ref, nw_ref, m_ref = (next(it) for _ in range(5))
    ws_ref = next(it) if has_small else None
    outs = [next(it) for _ in range(n_main_out)]
    ki_ref = next(it) if has_small else None
    small_ref = next(it) if has_small else None
    km_ref = next(it) if has_kmean else None
    hb_ref = next(it)
    j = pl.program_id(1)

    @pl.when(j == 0)
    def _():
        x = x_ref[...]
        h = x * lax.rsqrt(jnp.mean(x * x, axis=-1, keepdims=True) + EPS) * lnw_ref[...]
        hb = h.astype(BF16)
        hb_ref[...] = hb
        if has_small:
            s = _dg(hb, ws_ref[...], NN)
            lane = lax.broadcasted_iota(jnp.int32, s.shape, 1)
            is_ki = lane < IDX_DIM
            ms = jnp.sum(jnp.where(is_ki, s * s, 0.0), axis=-1, keepdims=True) / IDX_DIM
            sm = jnp.where(is_ki, s * lax.rsqrt(ms + EPS), s * (IDX_HEADS ** -0.5))
            small_ref[...] = sm
            ki_ref[...] = sm[:, :IDX_DIM]

    acc = _dg(hb_ref[...], w_ref[...], NN)
    for jj, (oi, off, nrow, kmean_here) in enumerate(plan):
        @pl.when(j == jj)
        def _(oi=oi, off=off, nrow=nrow, kmean_here=kmean_here):
            val = acc
            if nrow is not None:
                ms = _dot2(_split(acc * acc), m_ref[...])
                val = acc * lax.rsqrt(ms + EPS) * nw_ref[nrow:nrow + 1, :]
            outs[oi][:, off:off + GW] = val
            if kmean_here and has_kmean:
                for r in range(val.shape[0] // MOBA_BLOCK):
                    blk = val[r * MOBA_BLOCK:(r + 1) * MOBA_BLOCK, :]
                    km_ref[r, :, off:off + GW] = jnp.sum(blk, axis=0, keepdims=True) / MOBA_BLOCK


def _proj_call(x, ln_w, w_main, norm_w, mnorm, w_small, plan, out_widths, has_kmean):
    n = x.shape[0]
    tm = min(TM, n)
    assert n % tm == 0 and w_main.shape[1] == GW * len(plan)
    has_small = w_small is not None
    assert not has_kmean or tm % MOBA_BLOCK == 0
    row = lambda i, j: (i, 0)
    const = lambda i, j: (0, 0)
    in_specs = [
        pl.BlockSpec((tm, D_MODEL), row),
        pl.BlockSpec((1, D_MODEL), const),
        pl.BlockSpec((D_MODEL, GW), lambda i, j: (0, j)),
        pl.BlockSpec(norm_w.shape, const),
        pl.BlockSpec((GW, GW), const),
    ]
    args = [x, ln_w.reshape(1, D_MODEL), w_main, norm_w, mnorm]
    if has_small:
        in_specs.append(pl.BlockSpec((D_MODEL, SMALL_W), const))
        args.append(w_small)
    out_shape = [jax.ShapeDtypeStruct((n, w), F32) for w in out_widths]
    out_specs = [pl.BlockSpec((tm, w), row) for w in out_widths]
    if has_small:
        out_shape += [jax.ShapeDtypeStruct((n, IDX_DIM), F32), jax.ShapeDtypeStruct((n, SMALL_W), F32)]
        out_specs += [pl.BlockSpec((tm, IDX_DIM), row), pl.BlockSpec((tm, SMALL_W), row)]
    if has_kmean:
        out_shape.append(jax.ShapeDtypeStruct((n // MOBA_BLOCK, 1, C_WIDTH), F32))
        out_specs.append(pl.BlockSpec((tm // MOBA_BLOCK, 1, C_WIDTH), lambda i, j: (i, 0, 0)))
    kern = functools.partial(_proj_kernel, plan=tuple(plan), n_main_out=len(out_widths),
                             has_small=has_small, has_kmean=has_kmean)
    return pl.pallas_call(
        kern,
        grid=(n // tm, len(plan)),
        in_specs=in_specs,
        out_specs=out_specs,
        out_shape=out_shape,
        scratch_shapes=[pltpu.VMEM((tm, D_MODEL), BF16)],
        compiler_params=pltpu.CompilerParams(
            dimension_semantics=("parallel", "arbitrary"), vmem_limit_bytes=VMEM_LIMIT),
        name="proj_ab" if has_small else "proj_c",
    )(*args)


def _ab_weights(w_in, qn, kn):
    offs = np.cumsum([0, 512, 512, 512, 512, 512, 64, 8, 512, 512, 512, 512])
    qa, ka, va, ga, qi, ki, wi, qb, kb, vb, gb = (w_in[:, offs[k]:offs[k + 1]] for k in range(11))
    w_main = jnp.concatenate([ka, va, kb, vb, ga, gb, qa, qi, qb], axis=1)
    w_small = jnp.concatenate([ki, wi, jnp.zeros((D_MODEL, SMALL_W - IDX_DIM - IDX_HEADS), F32)], axis=1)
    norm_w = jnp.stack([jnp.tile(kn, N_HEADS), jnp.tile(qn, N_HEADS)])
    return w_main.astype(BF16), w_small.astype(BF16), norm_w


_AB_PLAN = [(0, 0, 0, False), (0, 512, None, False),
            (1, 0, None, False), (1, 512, None, False),
            (2, 0, None, False), (2, 512, None, False),
            (3, 0, 1, False),
            (4, 0, None, False),
            (5, 0, None, False)]
_AB_WIDTHS = [1024, 1024, 1024, 512, 512, 512]

_C_PLAN = [(0, 0, 0, True), (0, 512, 0, True),
           (0, 1024, None, False), (0, 1536, None, False),
           (1, 0, None, False), (1, 512, None, False),
           (2, 0, 1, False), (2, 512, 1, False)]
_C_WIDTHS = [2048, 1024, 1024]


def _c_weights(w_in, qn, kn):
    q, k, v, g = (w_in[:, c * C_WIDTH:(c + 1) * C_WIDTH] for c in range(4))
    w_main = jnp.concatenate([k, v, g, q], axis=1)
    norm_w = jnp.stack([jnp.tile(kn, GW // C_DIM), jnp.tile(qn, GW // C_DIM)])
    return w_main.astype(BF16), norm_w


def _out_kernel(*refs, n_o):
    x_ref, g_ref = refs[0], refs[1]
    o_refs = refs[2:2 + n_o]
    w_ref, y_ref = refs[2 + n_o:]
    g = g_ref[...]
    sg = g * (1.0 / (1.0 + jnp.exp(-g)))
    y = x_ref[...]
    off = 0
    for o_ref in o_refs:
        w = o_ref.shape[1]
        y = y + _dot1(o_ref[...] * sg[:, off:off + w], w_ref[off:off + w, :])
        off += w
    y_ref[...] = y


def _out_call(x, g, os_, w_out):
    n = x.shape[0]
    tm = min(TM, n)
    assert n % tm == 0
    row = lambda i: (i, 0)
    in_specs = [pl.BlockSpec((tm, D_MODEL), row), pl.BlockSpec((tm, D_MODEL), row)]
    in_specs += [pl.BlockSpec((tm, o.shape[1]), row) for o in os_]
    in_specs.append(pl.BlockSpec((D_MODEL, D_MODEL), lambda i: (0, 0)))
    return pl.pallas_call(
        functools.partial(_out_kernel, n_o=len(os_)),
        grid=(n // tm,),
        in_specs=in_specs,
        out_specs=pl.BlockSpec((tm, D_MODEL), row),
        out_shape=jax.ShapeDtypeStruct((n, D_MODEL), F32),
        compiler_params=pltpu.CompilerParams(
            dimension_semantics=("parallel",), vmem_limit_bytes=VMEM_LIMIT),
        name="out_proj",
    )(x, g, *os_, w_out)


def _tree(parts, comb):
    while len(parts) > 1:
        parts = [comb(parts[k], parts[k + 1]) for k in range(0, len(parts) - 1, 2)] + parts[len(parts) & ~1:]
    return parts[0]


def _fold_blocks(sc_ref, nblk, f, comb, init_val, key_axis):
    r, w = sc_ref.shape[1], sc_ref.shape[2]

    def vregs(x):
        if key_axis == 1:
            return _tree([x[:, c * LANES:(c + 1) * LANES] for c in range(w // LANES)], comb)
        return _tree([x[g * SUBLANES:(g + 1) * SUBLANES, :] for g in range(r // SUBLANES)], comb)

    init = jnp.full((r, LANES) if key_axis == 1 else (SUBLANES, w), init_val, F32)
    if isinstance(nblk, int):
        acc = init
        for j in range(nblk):
            acc = comb(acc, vregs(f(sc_ref[j], j)))
        return acc
    return lax.fori_loop(0, nblk, lambda j, acc: comb(acc, vregs(f(sc_ref[j], j))), init)


def _for_blocks(nblk, body):
    if isinstance(nblk, int):
        for j in range(nblk):
            body(j)
    else:
        lax.fori_loop(0, nblk, lambda j, c: (body(j), c)[1], 0)


def _topk_to_mask(sc_ref, nblk, k, key_axis):
    kf = float(k)
    keys_per_block = sc_ref.shape[1 + key_axis]
    red = -1 if key_axis == 1 else 0
    ones = lambda m: jnp.where(m, 1.0, 0.0)
    add, fmin, fmax = jnp.add, jnp.minimum, jnp.maximum
    fold = functools.partial(_fold_blocks, sc_ref, nblk, key_axis=key_axis)
    count = lambda pred: jnp.sum(fold(lambda s, j: ones(pred(s, j)), add, 0.0), axis=red, keepdims=True)

    nvalid = count(lambda s, j: s > NEG_INF)
    mn = jnp.min(fold(lambda s, j: jnp.where(s > NEG_INF, s, jnp.inf), fmin, jnp.inf), axis=red, keepdims=True)
    mx = jnp.max(fold(lambda s, j: s, fmax, NEG_INF), axis=red, keepdims=True)
    cmx = count(lambda s, j: s >= mx)
    cz_ge = count(lambda s, j: s >= 0.0)
    cz_gt = count(lambda s, j: s > 0.0)
    mpos = jnp.min(fold(lambda s, j: jnp.where(s > 0.0, s, jnp.inf), fmin, jnp.inf), axis=red, keepdims=True)
    few = nvalid <= kf
    top = cmx >= kf
    zero = (cz_gt < kf) & (cz_ge >= kf)
    v0 = jnp.where(few, NEG_INF, jnp.where(top, mx, 0.0))
    done0 = ones(few | top | zero)
    mn = jnp.where(cz_gt >= kf, mpos, mn)
    mx = jnp.where(cz_ge < kf, 0.0, mx)

    def cond(st):
        return jnp.min(st[3]) < 0.5

    def body(st):
        a, b, v, done = st
        p = a * 0.5 + b * 0.5
        stuck = (p <= a) | (p >= b)
        c = count(lambda s, j: s >= p)
        hit = c == kf
        is_done = done > 0.5
        v = jnp.where(is_done, v, jnp.where(stuck, a, jnp.where(hit, p, v)))
        frozen = is_done | stuck | hit
        a = jnp.where(frozen, a, jnp.where(c >= kf, p, a))
        b = jnp.where(frozen, b, jnp.where(c >= kf, b, p))
        return a, b, v, ones(frozen)

    _, _, v, _ = lax.while_loop(cond, body, (mn, mx, v0, done0))

    need = kf - count(lambda s, j: s > v)
    neq = count(lambda s, j: s == v)
    excess = (neq > need) & (v > NEG_INF)
    has_ties = jnp.max(ones(excess)) > 0.5
    key_index = lambda s, j: j * keys_per_block + lax.broadcasted_iota(jnp.int32, s.shape, key_axis)

    @pl.when(has_ties)
    def _():
        nbits = (sc_ref.shape[0] * keys_per_block).bit_length()
        cut = jnp.zeros(v.shape, jnp.int32)
        for bit in reversed(range(nbits)):
            cnd = cut + (1 << bit)
            cn = count(lambda s, j: (s == v) & (key_index(s, j) < cnd))
            cut = jnp.where(cn <= need, cnd, cut)

        def write(j):
            s = sc_ref[j]
            sel = ((s > v) | ((s == v) & (key_index(s, j) < cut))) & (s > NEG_INF)
            sc_ref[j] = jnp.where(sel, 0.0, NEG_INF)

        _for_blocks(nblk, write)

    @pl.when(jnp.logical_not(has_ties))
    def _():
        def write(j):
            s = sc_ref[j]
            sc_ref[j] = jnp.where((s >= v) & (s > NEG_INF), 0.0, NEG_INF)

        _for_blocks(nblk, write)


def _flash_init(m_ref, l_ref, acc_ref):
    m_ref[...] = jnp.full(m_ref.shape, M_INIT, F32)
    l_ref[...] = jnp.zeros(l_ref.shape, F32)
    acc_ref[...] = jnp.zeros(acc_ref.shape, F32)


def _flash_update(h, lg, vh, m_ref, l_ref, acc_ref):
    m = m_ref[h]
    m_new = jnp.maximum(m, jnp.max(lg, axis=0, keepdims=True))
    alpha = jnp.exp(m - m_new)
    p = jnp.exp(lg - m_new)
    m_ref[h] = m_new
    l_ref[h] = alpha * l_ref[h] + jnp.sum(p, axis=0, keepdims=True)
    acc_ref[h] = alpha * acc_ref[h] + _dot1(vh, p, TN)


def _flash_finish(o_ref, l_ref, acc_ref):
    for h in range(acc_ref.shape[0]):
        acc_ref[h] = acc_ref[h] / l_ref[h]
    o_ref[...] = acc_ref[...].reshape(acc_ref.shape[0] * acc_ref.shape[1], acc_ref.shape[2]).T


def _for_key_tiles(i, step):
    lax.fori_loop(0, jnp.maximum(i - 1, 0), lambda j, c: (step(j, "far"), c)[1], 0)

    @pl.when(i >= 1)
    def _():
        step(i - 1, "prev")

    step(i, "diag")


def _tile_bias(bandt_ref, far_ref, h, kind):
    if kind == "far":
        return far_ref[h][:, :1]
    return bandt_ref[h, :TQ, :] if kind == "prev" else bandt_ref[h, TQ:, :]


def _dsa_prompt_kernel(qa_ref, qi_ref, sm_ref, kv_ref, ki_ref, bandt_ref, far_ref, o_ref,
                       sc_ref, qb_ref, m_ref, l_ref, acc_ref, *, topk):
    i = pl.program_id(1)
    krow = lax.broadcasted_iota(jnp.int32, (TQ, TQ), 0)
    qcol = lax.broadcasted_iota(jnp.int32, (TQ, TQ), 1)
    wi_t = sm_ref[...].T[WI_LO:WI_HI, :]
    qb_ref[...] = qi_ref[...].astype(BF16)

    def score_body(j, carry):
        kid = ki_ref[pl.ds(pl.multiple_of(j * TQ, TQ), TQ), :].astype(BF16)
        sc = jnp.zeros((TQ, TQ), F32)
        for hh in range(IDX_HEADS):
            s = _dg(kid, qb_ref[:, hh * IDX_DIM:(hh + 1) * IDX_DIM], NT) * (IDX_DIM ** -0.5)
            sc = sc + wi_t[hh:hh + 1, :] * jnp.maximum(s, 0.0)
        causal = (krow + j * TQ) <= (qcol + i * TQ)
        sc_ref[j] = jnp.where(causal, sc, NEG_INF)
        return carry

    lax.fori_loop(0, i + 1, score_body, 0)
    _topk_to_mask(sc_ref, i + 1, topk, 0)

    qb_ref[...] = (qa_ref[...] * (AB_DIM ** -0.5)).astype(BF16)
    _flash_init(m_ref, l_ref, acc_ref)

    def step(j, kind):
        rows = pl.ds(pl.multiple_of(j * TQ, TQ), TQ)
        for h in range(N_HEADS):
            lanes = slice(h * AB_DIM, (h + 1) * AB_DIM)
            lg = _dot1(kv_ref[rows, lanes], qb_ref[:, lanes], NT)
            lg = lg + _tile_bias(bandt_ref, far_ref, h, kind) + sc_ref[j]
            vh = kv_ref[rows, AB_WIDTH + h * AB_DIM:AB_WIDTH + (h + 1) * AB_DIM]
            _flash_update(h, lg, vh, m_ref, l_ref, acc_ref)

    _for_key_tiles(i, step)
    _flash_finish(o_ref, l_ref, acc_ref)


def _dsa_prompt(qa, qi, small, kv_a, ki, band_t, far, batch, seq):
    nq = seq // TQ
    topk = min(DSA_TOPK, seq // 4)
    row_tile = lambda b, i: (b * nq + i, 0)
    whole = lambda b, i: (b, 0)
    return pl.pallas_call(
        functools.partial(_dsa_prompt_kernel, topk=topk),
        grid=(batch, nq),
        in_specs=[pl.BlockSpec((TQ, AB_WIDTH), row_tile),
                  pl.BlockSpec((TQ, AB_WIDTH), row_tile),
                  pl.BlockSpec((TQ, SMALL_W), row_tile),
                  pl.BlockSpec((seq, 2 * AB_WIDTH), whole),
                  pl.BlockSpec((seq, IDX_DIM), whole),
                  pl.BlockSpec((N_HEADS, 2 * TQ, TQ), lambda b, i: (0, 0, 0)),
                  pl.BlockSpec(far.shape, lambda b, i: (0, 0, 0))],
        out_specs=pl.BlockSpec((TQ, AB_WIDTH), row_tile),
        out_shape=jax.ShapeDtypeStruct((batch * seq, AB_WIDTH), F32),
        scratch_shapes=[pltpu.VMEM((nq, TQ, TQ), F32), pltpu.VMEM((TQ, AB_WIDTH), BF16),
                        pltpu.VMEM((N_HEADS, 1, TQ), F32), pltpu.VMEM((N_HEADS, 1, TQ), F32),
                        pltpu.VMEM((N_HEADS, AB_DIM, TQ), F32)],
        compiler_params=pltpu.CompilerParams(
            dimension_semantics=("parallel", "arbitrary"), vmem_limit_bytes=VMEM_LIMIT),
        name="dsa_prompt",
    )(qa, qi, small, kv_a, ki, band_t, far)


def _softplus_parts(z):
    t = jnp.log1p(jnp.exp(-jnp.abs(z)))
    return -(jnp.maximum(-z, 0.0) + t), -(jnp.maximum(z, 0.0) + t)


def _sb_prompt_kernel(q_ref, kv_ref, u_ref, o_ref, qb_ref, rs_ref, acc_ref):
    i = pl.program_id(1)
    row = lax.broadcasted_iota(jnp.int32, (TQ, TQ), 0)
    col = lax.broadcasted_iota(jnp.int32, (TQ, TQ), 1)
    qb_ref[...] = (q_ref[...] * (AB_DIM ** -0.5)).astype(BF16)
    rs_ref[...] = jnp.zeros(rs_ref.shape, F32)
    acc_ref[...] = jnp.zeros(acc_ref.shape, F32)

    def step(j, diag):
        rows = pl.ds(pl.multiple_of(j * TQ, TQ), TQ)
        for h in range(N_HEADS):
            lanes = slice(h * AB_DIM, (h + 1) * AB_DIM)
            z = _dot1(qb_ref[:, lanes], kv_ref[rows, lanes], NT)
            log_beta, log_1m = _softplus_parts(z)
            if diag:
                mask = col < row
                log_1m = jnp.where(mask, log_1m, 0.0)
            rs = rs_ref[h]
            suffix = _dot2(_split(log_1m), u_ref[...]) + rs
            w = jnp.exp(log_beta + suffix)
            if diag:
                w = jnp.where(mask, w, 0.0)
            vh = kv_ref[rows, AB_WIDTH + h * AB_DIM:AB_WIDTH + (h + 1) * AB_DIM]
            acc_ref[h] = acc_ref[h] + _dot1(w, vh)
            rs_ref[h] = rs + jnp.sum(log_1m, axis=-1, keepdims=True)

    step(i, True)
    lax.fori_loop(0, i, lambda t, c: (step(i - 1 - t, False), c)[1], 0)
    for h in range(N_HEADS):
        o_ref[:, h * AB_DIM:(h + 1) * AB_DIM] = acc_ref[h]


def _sb_prompt(qb, kv_b, batch, seq):
    nq = seq // TQ
    row_tile = lambda b, i: (b * nq + i, 0)
    return pl.pallas_call(
        _sb_prompt_kernel,
        grid=(batch, nq),
        in_specs=[pl.BlockSpec((TQ, AB_WIDTH), row_tile),
                  pl.BlockSpec((seq, 2 * AB_WIDTH), lambda b, i: (b, 0)),
                  pl.BlockSpec((TQ, TQ), lambda b, i: (0, 0))],
        out_specs=pl.BlockSpec((TQ, AB_WIDTH), row_tile),
        out_shape=jax.ShapeDtypeStruct((batch * seq, AB_WIDTH), F32),
        scratch_shapes=[pltpu.VMEM((TQ, AB_WIDTH), BF16), pltpu.VMEM((N_HEADS, TQ, 1), F32),
                        pltpu.VMEM((N_HEADS, TQ, AB_DIM), F32)],
        compiler_params=pltpu.CompilerParams(
            dimension_semantics=("parallel", "arbitrary"), vmem_limit_bytes=VMEM_LIMIT),
        name="sb_prompt",
    )(qb, kv_b, _strict_lower_ones(TQ))


def _moba_select(gate, n_valid, axis):
    nb = gate.shape[axis]
    blk = lax.broadcasted_iota(jnp.int32, gate.shape, axis)
    gate = jnp.where(blk < n_valid, gate, NEG_INF)
    sel = jnp.zeros(gate.shape, F32)
    for n in range(nb):
        gn = gate[n:n + 1, :] if axis == 0 else gate[:, n:n + 1]
        beats = (gate > gn) | ((gate == gn) & (blk < n))
        rank = jnp.sum(jnp.where(beats, 1.0, 0.0), axis=axis, keepdims=True)
        chosen = (rank < float(MOBA_TOPK)) & (blk == n) & (blk < n_valid)
        sel = jnp.where(chosen, 1.0, sel)
    return sel


def _moba_prompt_kernel(q_ref, k_ref, v_ref, km_ref, bandt_ref, far_ref, o_ref,
                        qb_ref, sel_ref, m_ref, l_ref, acc_ref):
    i = pl.program_id(2)
    hg = m_ref.shape[0]
    krow = lax.broadcasted_iota(jnp.int32, (TQ, TQ), 0)
    qcol = lax.broadcasted_iota(jnp.int32, (TQ, TQ), 1)
    blk = lax.broadcasted_iota(jnp.int32, sel_ref.shape[1:], 0)
    for h in range(hg):
        lanes = slice(h * C_DIM, (h + 1) * C_DIM)
        gate_t = _dot3(_split(km_ref[:, 0, lanes]), _split(q_ref[:, lanes]), NT)
        sel_ref[h] = _moba_select(gate_t, i, 0)
    qb_ref[...] = (q_ref[...] * (C_DIM ** -0.5)).astype(BF16)
    _flash_init(m_ref, l_ref, acc_ref)

    def step(j, kind):
        rows = pl.ds(pl.multiple_of(j * TQ, TQ), TQ)
        for h in range(hg):
            lanes = slice(h * C_DIM, (h + 1) * C_DIM)
            lg = _dot1(k_ref[rows, lanes], qb_ref[:, lanes], NT) + _tile_bias(bandt_ref, far_ref, h, kind)
            if kind == "diag":
                ok = krow <= qcol
            else:
                ok = jnp.sum(jnp.where(blk == j, sel_ref[h], 0.0), axis=0, keepdims=True) > 0.5
            lg = jnp.where(ok, lg, NEG_INF)
            _flash_update(h, lg, v_ref[rows, lanes], m_ref, l_ref, acc_ref)

    _for_key_tiles(i, step)
    _flash_finish(o_ref, l_ref, acc_ref)


def _moba_prompt(q, kv_c, kmean, band_t, far, batch, seq):
    nq = seq // TQ
    hg = MOBA_HEADS_PER_STEP
    gw = hg * C_DIM
    assert TQ == MOBA_BLOCK and N_HEADS % hg == 0
    q_tile = lambda b, g, i: (b * nq + i, g)
    return pl.pallas_call(
        _moba_prompt_kernel,
        grid=(batch, N_HEADS // hg, nq),
        in_specs=[pl.BlockSpec((TQ, gw), q_tile),
                  pl.BlockSpec((seq, gw), lambda b, g, i: (b, g)),
                  pl.BlockSpec((seq, gw), lambda b, g, i: (b, N_HEADS // hg + g)),
                  pl.BlockSpec((nq, 1, gw), lambda b, g, i: (b, 0, g)),
                  pl.BlockSpec((hg, 2 * TQ, TQ), lambda b, g, i: (g, 0, 0)),
                  pl.BlockSpec((hg, 1, LANES), lambda b, g, i: (g, 0, 0))],
        out_specs=pl.BlockSpec((TQ, gw), q_tile),
        out_shape=jax.ShapeDtypeStruct((batch * seq, C_WIDTH), F32),
        scratch_shapes=[pltpu.VMEM((TQ, gw), BF16), pltpu.VMEM((hg, nq, TQ), F32),
                        pltpu.VMEM((hg, 1, TQ), F32), pltpu.VMEM((hg, 1, TQ), F32),
                        pltpu.VMEM((hg, C_DIM, TQ), F32)],
        compiler_params=pltpu.CompilerParams(
            dimension_semantics=("parallel", "parallel", "arbitrary"), vmem_limit_bytes=VMEM_LIMIT),
        name="moba_prompt",
    )(q, kv_c, kv_c, kmean, band_t, far)


def _page_specs(n_pages, block, layer, tail=None):
    tail = (0,) * (len(block) - 2) if tail is None else tail
    return [pl.BlockSpec(block, lambda b, pt, p=p: (layer, pt[b, p]) + tail) for p in range(n_pages)]


def _pad_rows(x, rows):
    return jnp.concatenate([x, jnp.zeros((rows - x.shape[0], x.shape[1]), x.dtype)], axis=0)


def _head_rows(q, hm):
    return jnp.concatenate([q] * N_HEADS, axis=0) * hm


def _head_diag(res, hm, t):
    out = res[0:t, :] * hm[0:t, :]
    for h in range(1, N_HEADS):
        out = out + res[h * t:(h + 1) * t, :] * hm[h * t:(h + 1) * t, :]
    return out


def _sample_spec(t_new, width):
    return pl.BlockSpec((t_new, width), lambda b, pt: (b, 0))


def _new_kv_t(kv_new, batch, t_new):
    w = kv_new.shape[1] // 2
    x = kv_new.reshape(batch, t_new, 2, w).transpose(0, 2, 3, 1)
    return jnp.pad(x, ((0, 0), (0, 0), (0, 0), (0, PAGE - t_new)))


def _dsa_sample_kernel(pt_ref, qa_ref, qir_ref, sm_ref, kvn_ref, kin_ref, hm_ref, bias_ref, *rest,
                       n_pages, t_new, topk):
    kv_pages = rest[:n_pages] + (kvn_ref,)
    ki_pages = rest[n_pages:2 * n_pages] + (kin_ref,)
    o_ref, sc_ref, lg_ref = rest[2 * n_pages:]
    wi = sm_ref[:, WI_LO:WI_HI]
    qir = _split(qir_ref[...])
    lane = lax.broadcasted_iota(jnp.int32, (t_new, PAGE), 1)
    trow = lax.broadcasted_iota(jnp.int32, (t_new, PAGE), 0)

    for p in range(n_pages + 1):
        r = jnp.maximum(_dot3(qir, _split(ki_pages[p][...])) * (IDX_DIM ** -0.5), 0.0)
        sc = jnp.zeros((t_new, PAGE), F32)
        for hh in range(IDX_HEADS):
            sc = sc + wi[:, hh:hh + 1] * r[hh * t_new:(hh + 1) * t_new, :]
        if p == n_pages:
            sc = jnp.where(lane <= trow, sc, NEG_INF)
        sc_ref[p] = sc

    _topk_to_mask(sc_ref, n_pages + 1, topk, 1)

    hm = hm_ref[...]
    qrows = (_head_rows(qa_ref[...], hm) * (AB_DIM ** -0.5)).astype(BF16)
    m = jnp.full((N_HEADS * t_new, 1), M_INIT, F32)
    for p in range(n_pages + 1):
        mask = jnp.concatenate([sc_ref[p]] * N_HEADS, axis=0)
        lg = _dot1(qrows, kv_pages[p][0]) + bias_ref[p] + mask
        lg_ref[p] = lg
        m = jnp.maximum(m, jnp.max(lg, axis=-1, keepdims=True))
    l = jnp.zeros_like(m)
    acc = jnp.zeros((N_HEADS * t_new, AB_WIDTH), F32)
    for p in range(n_pages + 1):
        e = jnp.exp(lg_ref[p] - m)
        l = l + jnp.sum(e, axis=-1, keepdims=True)
        acc = acc + _dot1(e, kv_pages[p][1], NT)
    o_ref[...] = _head_diag(acc / l, hm, t_new)


def _dsa_sample(page_table, qa, qi_rows, small, kv_new_t, ki_new_t, hm, bias, cache_kv, cache_ki, layer, t_new):
    batch, n_pages = page_table.shape
    topk = min(DSA_TOPK, (n_pages * PAGE + t_new) // 4)
    in_specs = [_sample_spec(t_new, AB_WIDTH),
                pl.BlockSpec((None, IDX_HEADS * t_new, IDX_DIM), lambda b, pt: (b, 0, 0)),
                _sample_spec(t_new, SMALL_W),
                pl.BlockSpec((None, 2, AB_WIDTH, PAGE), lambda b, pt: (b, 0, 0, 0)),
                pl.BlockSpec((None, IDX_DIM, PAGE), lambda b, pt: (b, 0, 0)),
                pl.BlockSpec(hm.shape, lambda b, pt: (0, 0)),
                pl.BlockSpec(bias.shape, lambda b, pt: (0, 0, 0))]
    in_specs += _page_specs(n_pages, (None, None, 2, AB_WIDTH, PAGE), layer)
    in_specs += _page_specs(n_pages, (None, None, IDX_DIM, PAGE), layer)
    grid_spec = pltpu.PrefetchScalarGridSpec(
        num_scalar_prefetch=1, grid=(batch,), in_specs=in_specs,
        out_specs=_sample_spec(t_new, AB_WIDTH),
        scratch_shapes=[pltpu.VMEM((n_pages + 1, t_new, PAGE), F32),
                        pltpu.VMEM((n_pages + 1, N_HEADS * t_new, PAGE), F32)])
    return pl.pallas_call(
        functools.partial(_dsa_sample_kernel, n_pages=n_pages, t_new=t_new, topk=topk),
        grid_spec=grid_spec,
        out_shape=jax.ShapeDtypeStruct((batch * t_new, AB_WIDTH), F32),
        compiler_params=pltpu.CompilerParams(
            dimension_semantics=("arbitrary",), vmem_limit_bytes=VMEM_LIMIT),
        name="dsa_sample",
    )(page_table, qa, qi_rows, small, kv_new_t, ki_new_t, hm, bias,
      *([cache_kv] * n_pages), *([cache_ki] * n_pages))


def _sb_sample_kernel(pt_ref, q_ref, kvn_ref, hm_ref, u_ref, *rest, n_pages, t_new):
    kv_pages = rest[:n_pages] + (kvn_ref,)
    o_ref = rest[n_pages]
    hm = hm_ref[...]
    u = u_ref[...]
    qrows = (_head_rows(q_ref[...], hm) * (AB_DIM ** -0.5)).astype(BF16)
    shape = (N_HEADS * t_new, PAGE)
    lane = lax.broadcasted_iota(jnp.int32, shape, 1)
    trow = lax.rem(lax.broadcasted_iota(jnp.int32, shape, 0), t_new)
    rs = jnp.zeros((N_HEADS * t_new, 1), F32)
    acc = jnp.zeros((N_HEADS * t_new, AB_WIDTH), F32)
    for p in reversed(range(n_pages + 1)):
        z = _dot1(qrows, kv_pages[p][0])
        log_beta, log_1m = _softplus_parts(z)
        if p == n_pages:
            mask = lane < trow
            log_1m = jnp.where(mask, log_1m, 0.0)
        suffix = _dot2(_split(log_1m), u) + rs
        w = jnp.exp(log_beta + suffix)
        if p == n_pages:
            w = jnp.where(mask, w, 0.0)
        acc = acc + _dot1(w, kv_pages[p][1], NT)
        rs = rs + jnp.sum(log_1m, axis=-1, keepdims=True)
    o_ref[...] = _head_diag(acc, hm, t_new)


def _sb_sample(page_table, qb, kv_new_t, hm, cache_kv, layer, t_new):
    batch, n_pages = page_table.shape
    in_specs = [_sample_spec(t_new, AB_WIDTH),
                pl.BlockSpec((None, 2, AB_WIDTH, PAGE), lambda b, pt: (b, 0, 0, 0)),
                pl.BlockSpec(hm.shape, lambda b, pt: (0, 0)),
                pl.BlockSpec((PAGE, PAGE), lambda b, pt: (0, 0))]
    in_specs += _page_specs(n_pages, (None, None, 2, AB_WIDTH, PAGE), layer)
    grid_spec = pltpu.PrefetchScalarGridSpec(
        num_scalar_prefetch=1, grid=(batch,), in_specs=in_specs,
        out_specs=_sample_spec(t_new, AB_WIDTH))
    return pl.pallas_call(
        functools.partial(_sb_sample_kernel, n_pages=n_pages, t_new=t_new),
        grid_spec=grid_spec,
        out_shape=jax.ShapeDtypeStruct((batch * t_new, AB_WIDTH), F32),
        compiler_params=pltpu.CompilerParams(
            dimension_semantics=("arbitrary",), vmem_limit_bytes=VMEM_LIMIT),
        name="sb_sample",
    )(page_table, qb, kv_new_t, hm, _strict_lower_ones(PAGE), *([cache_kv] * n_pages))


def _moba_sample_kernel(pt_ref, q_ref, kvn_ref, bias_ref, *rest, n_pages, t_new):
    kv_pages = rest[:n_pages]
    o_ref, km_ref, lg_ref = rest[n_pages:]
    pages_per_block = MOBA_BLOCK // PAGE
    n_blocks = n_pages // pages_per_block
    rows_per_tok = 2 * N_HEADS
    for n in range(n_blocks):
        tot = jnp.zeros((rows_per_tok, C_DIM), F32)
        for r in range(pages_per_block):
            page = kv_pages[n * pages_per_block + r][...]
            tot = tot + jnp.sum(page.reshape(PAGE, rows_per_tok, C_DIM), axis=0)
        km_ref[n * N_HEADS:(n + 1) * N_HEADS, :] = tot[:N_HEADS, :] / MOBA_BLOCK
    lane = lax.broadcasted_iota(jnp.int32, (t_new, PAGE), 1)
    trow = lax.broadcasted_iota(jnp.int32, (t_new, PAGE), 0)
    kvn = _pad_rows(kvn_ref[...], PAGE)
    for h in range(N_HEADS):
        q = q_ref[:, h * C_DIM:(h + 1) * C_DIM]
        km_h = km_ref[pl.ds(h, n_blocks, stride=N_HEADS), :]
        gate = _dot3(_split(q), _split(km_h), NT)
        sel = _moba_select(gate, n_blocks, 1)
        qs = (q * (C_DIM ** -0.5)).astype(BF16)
        m = jnp.full((t_new, 1), M_INIT, F32)
        for p in range(n_pages + 1):
            if p < n_pages:
                k = kv_pages[p][pl.ds(h, PAGE, stride=rows_per_tok), :]
                n = p // pages_per_block
                ok = sel[:, n:n + 1] > 0.5
            else:
                k = kvn[:, h * C_DIM:(h + 1) * C_DIM]
                ok = lane <= trow
            lg = jnp.where(ok, _dot1(qs, k, NT) + bias_ref[p, h * t_new:(h + 1) * t_new, :], NEG_INF)
            lg_ref[p] = lg
            m = jnp.maximum(m, jnp.max(lg, axis=-1, keepdims=True))
        l = jnp.zeros_like(m)
        acc = jnp.zeros((t_new, C_DIM), F32)
        for p in range(n_pages + 1):
            if p < n_pages:
                v = kv_pages[p][pl.ds(N_HEADS + h, PAGE, stride=rows_per_tok), :]
            else:
                v = kvn[:, C_WIDTH + h * C_DIM:C_WIDTH + (h + 1) * C_DIM]
            e = jnp.exp(lg_ref[p] - m)
            l = l + jnp.sum(e, axis=-1, keepdims=True)
            acc = acc + _dot1(e, v)
        o_ref[:, h * C_DIM:(h + 1) * C_DIM] = acc / l


def _moba_sample(page_table, q, kv_new, bias, cache_kv, layer, t_new):
    batch, n_pages = page_table.shape
    assert (n_pages * PAGE) % MOBA_BLOCK == 0 and t_new <= MOBA_BLOCK
    in_specs = [_sample_spec(t_new, C_WIDTH),
                _sample_spec(t_new, 2 * C_WIDTH),
                pl.BlockSpec(bias.shape, lambda b, pt: (0, 0, 0))]
    in_specs += _page_specs(n_pages, (None, None, PAGE * 2 * N_HEADS, C_DIM), layer)
    grid_spec = pltpu.PrefetchScalarGridSpec(
        num_scalar_prefetch=1, grid=(batch,), in_specs=in_specs,
        out_specs=_sample_spec(t_new, C_WIDTH),
        scratch_shapes=[pltpu.VMEM((n_pages * PAGE // MOBA_BLOCK * N_HEADS, C_DIM), F32),
                        pltpu.VMEM((n_pages + 1, t_new, PAGE), F32)])
    return pl.pallas_call(
        functools.partial(_moba_sample_kernel, n_pages=n_pages, t_new=t_new),
        grid_spec=grid_spec,
        out_shape=jax.ShapeDtypeStruct((batch * t_new, C_WIDTH), F32),
        compiler_params=pltpu.CompilerParams(
            dimension_semantics=("arbitrary",), vmem_limit_bytes=VMEM_LIMIT),
        name="moba_sample",
    )(page_table, q, kv_new, bias, *([cache_kv] * n_pages))


def kernel(x_prompt, x_sample, cache_a_kv, cache_a_kidx, cache_b_kv, cache_c_kv, page_table, ln_w,
           w_in_ab, w_out_ab, qn_a, kn_a, w_in_c, w_out_c, qn_c, kn_c, rel_bias):
    batch, seq, _ = x_prompt.shape
    dec_batch, t_new, _ = x_sample.shape
    n_pages = page_table.shape[1]
    past_len = n_pages * PAGE
    depth = ln_w.shape[0]
    assert cache_a_kv.shape[2] == PAGE and seq % TQ == 0

    xp = x_prompt.reshape(batch * seq, D_MODEL)
    xs = x_sample.reshape(dec_batch * t_new, D_MODEL)
    n_pool = cache_a_kv.shape[1]
    ca_kv = jnp.transpose(cache_a_kv, (0, 1, 3, 4, 5, 2)).reshape(-1, n_pool, 2, AB_WIDTH, PAGE)
    cb_kv = jnp.transpose(cache_b_kv, (0, 1, 3, 4, 5, 2)).reshape(-1, n_pool, 2, AB_WIDTH, PAGE)
    ca_ki = jnp.transpose(cache_a_kidx, (0, 1, 3, 2))
    cc_kv = cache_c_kv.reshape(-1, n_pool, PAGE * 2 * N_HEADS, C_DIM)

    band_t, far = _prompt_bias(rel_bias)
    bias_s = _sample_bias(rel_bias, past_len, t_new)
    hm_ab = _head_mask(AB_WIDTH, t_new)
    mnorm_ab = _block_mean_matrix(GW, AB_DIM)
    mnorm_c = _block_mean_matrix(GW, C_DIM)

    outs = {k: [] for k in ("a_kv_p", "a_kv_s", "a_ki_p", "a_ki_s", "b_kv_p", "b_kv_s", "c_kv_p", "c_kv_s")}
    for l in range(depth):
        i = l // 2
        if l % 2 == 0:
            w_main, w_small, norm_w = _ab_weights(w_in_ab[i], qn_a[i], kn_a[i])
            w_out = w_out_ab[i].astype(BF16)
            proj = lambda x: _proj_call(x, ln_w[l], w_main, norm_w, mnorm_ab, w_small,
                                        _AB_PLAN, _AB_WIDTHS, False)
            kv_a, kv_b, g, qa, qi, qb, ki, small = proj(xp)
            oa = _dsa_prompt(qa, qi, small, kv_a, ki, band_t, far, batch, seq)
            ob = _sb_prompt(qb, kv_b, batch, seq)
            xp = _out_call(xp, g, [oa, ob], w_out)
            outs["a_kv_p"].append(kv_a)
            outs["a_ki_p"].append(ki)
            outs["b_kv_p"].append(kv_b)

            kv_a, kv_b, g, qa, qi, qb, ki, small = proj(xs)
            qi_rows = qi.reshape(dec_batch, t_new, IDX_HEADS, IDX_DIM).transpose(0, 2, 1, 3)
            qi_rows = qi_rows.reshape(dec_batch, IDX_HEADS * t_new, IDX_DIM)
            ki_t = jnp.pad(ki.reshape(dec_batch, t_new, IDX_DIM).transpose(0, 2, 1),
                           ((0, 0), (0, 0), (0, PAGE - t_new)))
            oa = _dsa_sample(page_table, qa, qi_rows, small, _new_kv_t(kv_a, dec_batch, t_new), ki_t,
                             hm_ab, bias_s, ca_kv, ca_ki, i, t_new)
            ob = _sb_sample(page_table, qb, _new_kv_t(kv_b, dec_batch, t_new), hm_ab, cb_kv, i, t_new)
            xs = _out_call(xs, g, [oa, ob], w_out)
            outs["a_kv_s"].append(kv_a)
            outs["a_ki_s"].append(ki)
            outs["b_kv_s"].append(kv_b)
        else:
            w_main, norm_w = _c_weights(w_in_c[i], qn_c[i], kn_c[i])
            w_out = w_out_c[i].astype(BF16)
            proj = lambda x, km: _proj_call(x, ln_w[l], w_main, norm_w, mnorm_c, None,
                                            _C_PLAN, _C_WIDTHS, km)
            kv_c, g, q, kmean = proj(xp, True)
            o = _moba_prompt(q, kv_c, kmean, band_t, far, batch, seq)
            xp = _out_call(xp, g, [o], w_out)
            outs["c_kv_p"].append(kv_c)

            kv_c, g, q = proj(xs, False)
            o = _moba_sample(page_table, q, kv_c, bias_s, cc_kv, i, t_new)
            xs = _out_call(xs, g, [o], w_out)
            outs["c_kv_s"].append(kv_c)

    def kv(name, b, t, d):
        return jnp.stack(outs[name]).reshape(-1, b, t, 2, N_HEADS, d)

    return (xp.reshape(batch, seq, D_MODEL), xs.reshape(dec_batch, t_new, D_MODEL),
            kv("a_kv_p", batch, seq, AB_DIM), kv("a_kv_s", dec_batch, t_new, AB_DIM),
            jnp.stack(outs["a_ki_p"]).reshape(-1, batch, seq, IDX_DIM),
            jnp.stack(outs["a_ki_s"]).reshape(-1, dec_batch, t_new, IDX_DIM),
            kv("b_kv_p", batch, seq, AB_DIM), kv("b_kv_s", dec_batch, t_new, AB_DIM),
            kv("c_kv_p", batch, seq, C_DIM), kv("c_kv_s", dec_batch, t_new, C_DIM))
```

```python
import functools
import math

import numpy as np
import jax
import jax.numpy as jnp
from jax import lax
from jax.experimental import pallas as pl
from jax.experimental.pallas import tpu as pltpu

F32 = jnp.float32
BF16 = jnp.bfloat16
HI = lax.Precision.HIGHEST
NEG_INF = float("-inf")
M_INIT = -1e30

D_MODEL = 1024
N_HEADS = 8
AB_DIM = 64
AB_WIDTH = N_HEADS * AB_DIM
C_DIM = 128
C_WIDTH = N_HEADS * C_DIM
IDX_HEADS = 8
IDX_DIM = 64
DSA_TOPK = 256
MOBA_BLOCK = 256
MOBA_TOPK = 3
PAGE = 128
REL_BUCKETS = 32
REL_MAX_DIST = 128
EPS = 1e-6

LANES = 128
SUBLANES = 8
TQ = MOBA_BLOCK
GW = 512
TM = 512
MOBA_HEADS_PER_STEP = 4
SMALL_W = LANES
WI_LO, WI_HI = IDX_DIM, IDX_DIM + IDX_HEADS
NN = (((1,), (0,)), ((), ()))
NT = (((1,), (1,)), ((), ()))
TN = (((0,), (0,)), ((), ()))
VMEM_LIMIT = 48 * 1024 * 1024


def _split(x):
    hi = x.astype(BF16)
    return hi, (x - hi.astype(F32)).astype(BF16)


def _dg(a, b, dims):
    return lax.dot_general(a, b, dims, preferred_element_type=F32)


def _dot3(a, b, dims=NN):
    return _dg(a[0], b[0], dims) + _dg(a[0], b[1], dims) + _dg(a[1], b[0], dims)


def _dot2(a, b_exact, dims=NN):
    return _dg(a[0], b_exact, dims) + _dg(a[1], b_exact, dims)


def _dot1(a, b, dims=NN):
    return _dg(a.astype(BF16), b.astype(BF16), dims)


def _t5_bucket_np(dist):
    dist = np.maximum(np.asarray(dist, np.int64), 0)
    max_exact = REL_BUCKETS // 2
    d = np.maximum(dist, max_exact).astype(np.float64)
    val = np.log(d / max_exact) / math.log(REL_MAX_DIST / max_exact) * (REL_BUCKETS - max_exact)
    large = np.minimum(max_exact + val.astype(np.int64), REL_BUCKETS - 1)
    return np.where(dist < max_exact, dist, large).astype(np.int32)


def _bias_lookup(rel_bias, buckets):
    onehot = (jnp.asarray(buckets.reshape(-1))[:, None] == jnp.arange(REL_BUCKETS)[None, :]).astype(F32)
    out = jnp.dot(onehot, rel_bias.astype(F32), precision=HI)
    return out.T.reshape((N_HEADS,) + buckets.shape)


def _prompt_bias(rel_bias):
    c = np.arange(2 * TQ)[:, None]
    r = np.arange(TQ)[None, :]
    band = _bias_lookup(rel_bias, _t5_bucket_np(TQ + r - c))
    far_bucket = _t5_bucket_np(np.array([TQ + 1, 1 << 20]))
    assert far_bucket[0] == far_bucket[1]
    far = jnp.broadcast_to(rel_bias[int(far_bucket[0])].astype(F32)[:, None, None], (N_HEADS, 1, LANES))
    return band, far


def _sample_bias(rel_bias, past_len, t_new):
    n_pages = past_len // PAGE
    p = np.arange(n_pages + 1)[:, None, None]
    t = np.arange(t_new)[None, :, None]
    c = np.arange(PAGE)[None, None, :]
    b = _bias_lookup(rel_bias, _t5_bucket_np(past_len + t - (p * PAGE + c)))
    return jnp.transpose(b, (1, 0, 2, 3)).reshape(n_pages + 1, N_HEADS * t_new, PAGE)


def _head_mask(width, rows_per_head):
    d = width // N_HEADS
    h_row = np.arange(N_HEADS * rows_per_head)[:, None] // rows_per_head
    h_col = np.arange(width)[None, :] // d
    return jnp.asarray((h_row == h_col).astype(np.float32))


def _block_mean_matrix(width, d):
    g = np.arange(width) // d
    return jnp.asarray((g[:, None] == g[None, :]).astype(np.float32) / d).astype(BF16)


def _strict_lower_ones(n):
    idx = np.arange(n)
    return jnp.asarray((idx[:, None] > idx[None, :]).astype(np.float32)).astype(BF16)


def _proj_kernel(*refs, plan, n_main_out, has_small, has_kmean):
    it = iter(refs)
    x_ref, lnw_ref, w_ref, nw_ref, m_ref = (next(it) for _ in range(5))
    ws_ref = next(it) if has_small else None
    outs = [next(it) for _ in range(n_main_out)]
    ki_ref = next(it) if has_small else None
    small_ref = next(it) if has_small else None
    km_ref = next(it) if has_kmean else None
    hb_ref = next(it)
    j = pl.program_id(1)

    @pl.when(j == 0)
    def _():
        x = x_ref[...]
        h = x * lax.rsqrt(jnp.mean(x * x, axis=-1, keepdims=True) + EPS) * lnw_ref[...]
        hb = h.astype(BF16)
        hb_ref[...] = hb
        if has_small:
            s = _dg(hb, ws_ref[...], NN)
            lane = lax.broadcasted_iota(jnp.int32, s.shape, 1)
            is_ki = lane < IDX_DIM
            ms = jnp.sum(jnp.where(is_ki, s * s, 0.0), axis=-1, keepdims=True) / IDX_DIM
            sm = jnp.where(is_ki, s * lax.rsqrt(ms + EPS), s * (IDX_HEADS ** -0.5))
            small_ref[...] = sm
            ki_ref[...] = sm[:, :IDX_DIM]

    acc = _dg(hb_ref[...], w_ref[...], NN)
    for jj, (oi, off, nrow, kmean_here) in enumerate(plan):
        @pl.when(j == jj)
        def _(oi=oi, off=off, nrow=nrow, kmean_here=kmean_here):
            val = acc
            if nrow is not None:
                ms = _dot2(_split(acc * acc), m_ref[...])
                val = acc * lax.rsqrt(ms + EPS) * nw_ref[nrow:nrow + 1, :]
            outs[oi][:, off:off + GW] = val
            if kmean_here and has_kmean:
                for r in range(val.shape[0] // MOBA_BLOCK):
                    blk = val[r * MOBA_BLOCK:(r + 1) * MOBA_BLOCK, :]
                    km_ref[r, :, off:off + GW] = jnp.sum(blk, axis=0, keepdims=True) / MOBA_BLOCK


def _proj_call(x, ln_w, w_main, norm_w, mnorm, w_small, plan, out_widths, has_kmean):
    n = x.shape[0]
    tm = min(TM, n)
    assert n % tm == 0 and w_main.shape[1] == GW * len(plan)
    has_small = w_small is not None
    assert not has_kmean or tm % MOBA_BLOCK == 0
    row = lambda i, j: (i, 0)
    const = lambda i, j: (0, 0)
    in_specs = [
        pl.BlockSpec((tm, D_MODEL), row),
        pl.BlockSpec((1, D_MODEL), const),
        pl.BlockSpec((D_MODEL, GW), lambda i, j: (0, j)),
        pl.BlockSpec(norm_w.shape, const),
        pl.BlockSpec((GW, GW), const),
    ]
    args = [x, ln_w.reshape(1, D_MODEL), w_main, norm_w, mnorm]
    if has_small:
        in_specs.append(pl.BlockSpec((D_MODEL, SMALL_W), const))
        args.append(w_small)
    out_shape = [jax.ShapeDtypeStruct((n, w), F32) for w in out_widths]
    out_specs = [pl.BlockSpec((tm, w), row) for w in out_widths]
    if has_small:
        out_shape += [jax.ShapeDtypeStruct((n, IDX_DIM), F32), jax.ShapeDtypeStruct((n, SMALL_W), F32)]
        out_specs += [pl.BlockSpec((tm, IDX_DIM), row), pl.BlockSpec((tm, SMALL_W), row)]
    if has_kmean:
        out_shape.append(jax.ShapeDtypeStruct((n // MOBA_BLOCK, 1, C_WIDTH), F32))
        out_specs.append(pl.BlockSpec((tm // MOBA_BLOCK, 1, C_WIDTH), lambda i, j: (i, 0, 0)))
    kern = functools.partial(_proj_kernel, plan=tuple(plan), n_main_out=len(out_widths),
                             has_small=has_small, has_kmean=has_kmean)
    return pl.pallas_call(
        kern,
        grid=(n // tm, len(plan)),
        in_specs=in_specs,
        out_specs=out_specs,
        out_shape=out_shape,
        scratch_shapes=[pltpu.VMEM((tm, D_MODEL), BF16)],
        compiler_params=pltpu.CompilerParams(
            dimension_semantics=("parallel", "arbitrary"), vmem_limit_bytes=VMEM_LIMIT),
        name="proj_ab" if has_small else "proj_c",
    )(*args)


def _ab_weights(w_in, qn, kn):
    offs = np.cumsum([0, 512, 512, 512, 512, 512, 64, 8, 512, 512, 512, 512])
    qa, ka, va, ga, qi, ki, wi, qb, kb, vb, gb = (w_in[:, offs[k]:offs[k + 1]] for k in range(11))
    w_main = jnp.concatenate([ka, va, kb, vb, ga, gb, qa, qi, qb], axis=1)
    w_small = jnp.concatenate([ki, wi, jnp.zeros((D_MODEL, SMALL_W - IDX_DIM - IDX_HEADS), F32)], axis=1)
    norm_w = jnp.stack([jnp.tile(kn, N_HEADS), jnp.tile(qn, N_HEADS)])
    return w_main.astype(BF16), w_small.astype(BF16), norm_w


_AB_PLAN = [(0, 0, 0, False), (0, 512, None, False),
            (1, 0, None, False), (1, 512, None, False),
            (2, 0, None, False), (2, 512, None, False),
            (3, 0, 1, False),
            (4, 0, None, False),
            (5, 0, None, False)]
_AB_WIDTHS = [1024, 1024, 1024, 512, 512, 512]

_C_PLAN = [(0, 0, 0, True), (0, 512, 0, True),
           (0, 1024, None, False), (0, 1536, None, False),
           (1, 0, None, False), (1, 512, None, False),
           (2, 0, 1, False), (2, 512, 1, False)]
_C_WIDTHS = [2048, 1024, 1024]


def _c_weights(w_in, qn, kn):
    q, k, v, g = (w_in[:, c * C_WIDTH:(c + 1) * C_WIDTH] for c in range(4))
    w_main = jnp.concatenate([k, v, g, q], axis=1)
    norm_w = jnp.stack([jnp.tile(kn, GW // C_DIM), jnp.tile(qn, GW // C_DIM)])
    return w_main.astype(BF16), norm_w


def _out_kernel(*refs, n_o):
    x_ref, g_ref = refs[0], refs[1]
    o_refs = refs[2:2 + n_o]
    w_ref, y_ref = refs[2 + n_o:]
    g = g_ref[...]
    sg = g * (1.0 / (1.0 + jnp.exp(-g)))
    y = x_ref[...]
    off = 0
    for o_ref in o_refs:
        w = o_ref.shape[1]
        y = y + _dot1(o_ref[...] * sg[:, off:off + w], w_ref[off:off + w, :])
        off += w
    y_ref[...] = y


def _out_call(x, g, os_, w_out):
    n = x.shape[0]
    tm = min(TM, n)
    assert n % tm == 0
    row = lambda i: (i, 0)
    in_specs = [pl.BlockSpec((tm, D_MODEL), row), pl.BlockSpec((tm, D_MODEL), row)]
    in_specs += [pl.BlockSpec((tm, o.shape[1]), row) for o in os_]
    in_specs.append(pl.BlockSpec((D_MODEL, D_MODEL), lambda i: (0, 0)))
    return pl.pallas_call(
        functools.partial(_out_kernel, n_o=len(os_)),
        grid=(n // tm,),
        in_specs=in_specs,
        out_specs=pl.BlockSpec((tm, D_MODEL), row),
        out_shape=jax.ShapeDtypeStruct((n, D_MODEL), F32),
        compiler_params=pltpu.CompilerParams(
            dimension_semantics=("parallel",), vmem_limit_bytes=VMEM_LIMIT),
        name="out_proj",
    )(x, g, *os_, w_out)


def _tree(parts, comb):
    while len(parts) > 1:
        parts = [comb(parts[k], parts[k + 1]) for k in range(0, len(parts) - 1, 2)] + parts[len(parts) & ~1:]
    return parts[0]


def _fold_blocks(sc_ref, nblk, f, comb, init_val, key_axis):
    r, w = sc_ref.shape[1], sc_ref.shape[2]

    def vregs(x):
        if key_axis == 1:
            return _tree([x[:, c * LANES:(c + 1) * LANES] for c in range(w // LANES)], comb)
        return _tree([x[g * SUBLANES:(g + 1) * SUBLANES, :] for g in range(r // SUBLANES)], comb)

    init = jnp.full((r, LANES) if key_axis == 1 else (SUBLANES, w), init_val, F32)
    if isinstance(nblk, int):
        acc = init
        for j in range(nblk):
            acc = comb(acc, vregs(f(sc_ref[j], j)))
        return acc
    return lax.fori_loop(0, nblk, lambda j, acc: comb(acc, vregs(f(sc_ref[j], j))), init)


def _for_blocks(nblk, body):
    if isinstance(nblk, int):
        for j in range(nblk):
            body(j)
    else:
        lax.fori_loop(0, nblk, lambda j, c: (body(j), c)[1], 0)


def _topk_to_mask(sc_ref, nblk, k, key_axis):
    kf = float(k)
    keys_per_block = sc_ref.shape[1 + key_axis]
    red = -1 if key_axis == 1 else 0
    ones = lambda m: jnp.where(m, 1.0, 0.0)
    add, fmin, fmax = jnp.add, jnp.minimum, jnp.maximum
    fold = functools.partial(_fold_blocks, sc_ref, nblk, key_axis=key_axis)
    count = lambda pred: jnp.sum(fold(lambda s, j: ones(pred(s, j)), add, 0.0), axis=red, keepdims=True)

    nvalid = count(lambda s, j: s > NEG_INF)
    mn = jnp.min(fold(lambda s, j: jnp.where(s > NEG_INF, s, jnp.inf), fmin, jnp.inf), axis=red, keepdims=True)
    mx = jnp.max(fold(lambda s, j: s, fmax, NEG_INF), axis=red, keepdims=True)
    cmx = count(lambda s, j: s >= mx)
    cz_ge = count(lambda s, j: s >= 0.0)
    cz_gt = count(lambda s, j: s > 0.0)
    mpos = jnp.min(fold(lambda s, j: jnp.where(s > 0.0, s, jnp.inf), fmin, jnp.inf), axis=red, keepdims=True)
    few = nvalid <= kf
    top = cmx >= kf
    zero = (cz_gt < kf) & (cz_ge >= kf)
    v0 = jnp.where(few, NEG_INF, jnp.where(top, mx, 0.0))
    done0 = ones(few | top | zero)
    ca0 = jnp.where(cz_gt >= kf, cz_gt, nvalid)
    cb0 = jnp.where(cz_ge < kf, cz_ge, cmx)
    mn = jnp.where(cz_gt >= kf, mpos, mn)
    mx = jnp.where(cz_ge < kf, 0.0, mx)

    def cond(st):
        a, b, ca, cb, v, done = st
        return jnp.min(jnp.maximum(done, ones(ca - cb <= 2.0))) < 0.5

    def body(st):
        a, b, ca, cb, v, done = st
        p = a * 0.5 + b * 0.5
        stuck = (p <= a) | (p >= b)
        c = count(lambda s, j: s >= p)
        hit = c == kf
        is_done = done > 0.5
        v = jnp.where(is_done, v, jnp.where(stuck, a, jnp.where(hit, p, v)))
        frozen = is_done | stuck | hit
        low = c >= kf
        a, ca = jnp.where(frozen | ~low, a, p), jnp.where(frozen | ~low, ca, c)
        b, cb = jnp.where(frozen | low, b, p), jnp.where(frozen | low, cb, c)
        return a, b, ca, cb, v, ones(frozen)

    a, b, ca, cb, v, done = lax.while_loop(cond, body, (mn, mx, ca0, cb0, v0, done0))
    in_ab = lambda s: (s >= a) & (s < b)
    hi = jnp.max(fold(lambda s, j: jnp.where(in_ab(s), s, NEG_INF), fmax, NEG_INF), axis=red, keepdims=True)
    lo = jnp.min(fold(lambda s, j: jnp.where(in_ab(s), s, jnp.inf), fmin, jnp.inf), axis=red, keepdims=True)
    v = jnp.where(done > 0.5, v, jnp.where(kf - cb <= 1.0, hi, lo))

    need = kf - count(lambda s, j: s > v)
    neq = count(lambda s, j: s == v)
    excess = (neq > need) & (v > NEG_INF)
    has_ties = jnp.max(ones(excess)) > 0.5
    key_index = lambda s, j: j * keys_per_block + lax.broadcasted_iota(jnp.int32, s.shape, key_axis)

    @pl.when(has_ties)
    def _():
        nbits = (sc_ref.shape[0] * keys_per_block).bit_length()
        cut = jnp.zeros(v.shape, jnp.int32)
        for bit in reversed(range(nbits)):
            cnd = cut + (1 << bit)
            cn = count(lambda s, j: (s == v) & (key_index(s, j) < cnd))
            cut = jnp.where(cn <= need, cnd, cut)

        def write(j):
            s = sc_ref[j]
            sel = ((s > v) | ((s == v) & (key_index(s, j) < cut))) & (s > NEG_INF)
            sc_ref[j] = jnp.where(sel, 0.0, NEG_INF)

        _for_blocks(nblk, write)

    @pl.when(jnp.logical_not(has_ties))
    def _():
        def write(j):
            s = sc_ref[j]
            sc_ref[j] = jnp.where((s >= v) & (s > NEG_INF), 0.0, NEG_INF)

        _for_blocks(nblk, write)


def _flash_init(m_ref, l_ref, acc_ref):
    m_ref[...] = jnp.full(m_ref.shape, M_INIT, F32)
    l_ref[...] = jnp.zeros(l_ref.shape, F32)
    acc_ref[...] = jnp.zeros(acc_ref.shape, F32)


def _flash_update(h, lg, vh, m_ref, l_ref, acc_ref):
    m = m_ref[h]
    m_new = jnp.maximum(m, jnp.max(lg, axis=0, keepdims=True))
    alpha = jnp.exp(m - m_new)
    p = jnp.exp(lg - m_new)
    m_ref[h] = m_new
    l_ref[h] = alpha * l_ref[h] + jnp.sum(p, axis=0, keepdims=True)
    acc_ref[h] = alpha * acc_ref[h] + _dot1(vh, p, TN)


def _flash_finish(o_ref, l_ref, acc_ref):
    for h in range(acc_ref.shape[0]):
        acc_ref[h] = acc_ref[h] / l_ref[h]
    o_ref[...] = acc_ref[...].reshape(acc_ref.shape[0] * acc_ref.shape[1], acc_ref.shape[2]).T


def _for_key_tiles(i, step):
    lax.fori_loop(0, jnp.maximum(i - 1, 0), lambda j, c: (step(j, "far"), c)[1], 0)

    @pl.when(i >= 1)
    def _():
        step(i - 1, "prev")

    step(i, "diag")


def _tile_bias(bandt_ref, far_ref, h, kind):
    if kind == "far":
        return far_ref[h][:, :1]
    return bandt_ref[h, :TQ, :] if kind == "prev" else bandt_ref[h, TQ:, :]


def _dsa_prompt_kernel(qa_ref, qi_ref, sm_ref, kv_ref, ki_ref, bandt_ref, far_ref, o_ref,
                       sc_ref, qb_ref, m_ref, l_ref, acc_ref, *, topk):
    i = pl.program_id(1)
    krow = lax.broadcasted_iota(jnp.int32, (TQ, TQ), 0)
    qcol = lax.broadcasted_iota(jnp.int32, (TQ, TQ), 1)
    wi_t = sm_ref[...].T[WI_LO:WI_HI, :]
    qb_ref[...] = qi_ref[...].astype(BF16)

    def score_body(j, carry):
        kid = ki_ref[pl.ds(pl.multiple_of(j * TQ, TQ), TQ), :].astype(BF16)
        sc = jnp.zeros((TQ, TQ), F32)
        for hh in range(IDX_HEADS):
            s = _dg(kid, qb_ref[:, hh * IDX_DIM:(hh + 1) * IDX_DIM], NT) * (IDX_DIM ** -0.5)
            sc = sc + wi_t[hh:hh + 1, :] * jnp.maximum(s, 0.0)
        causal = (krow + j * TQ) <= (qcol + i * TQ)
        sc_ref[j] = jnp.where(causal, sc, NEG_INF)
        return carry

    lax.fori_loop(0, i + 1, score_body, 0)
    _topk_to_mask(sc_ref, i + 1, topk, 0)

    qb_ref[...] = (qa_ref[...] * (AB_DIM ** -0.5)).astype(BF16)
    _flash_init(m_ref, l_ref, acc_ref)

    def step(j, kind):
        rows = pl.ds(pl.multiple_of(j * TQ, TQ), TQ)
        for h in range(N_HEADS):
            lanes = slice(h * AB_DIM, (h + 1) * AB_DIM)
            lg = _dot1(kv_ref[rows, lanes], qb_ref[:, lanes], NT)
            lg = lg + _tile_bias(bandt_ref, far_ref, h, kind) + sc_ref[j]
            vh = kv_ref[rows, AB_WIDTH + h * AB_DIM:AB_WIDTH + (h + 1) * AB_DIM]
            _flash_update(h, lg, vh, m_ref, l_ref, acc_ref)

    _for_key_tiles(i, step)
    _flash_finish(o_ref, l_ref, acc_ref)


def _dsa_prompt(qa, qi, small, kv_a, ki, band_t, far, batch, seq):
    nq = seq // TQ
    topk = min(DSA_TOPK, seq // 4)
    row_tile = lambda b, i: (b * nq + i, 0)
    whole = lambda b, i: (b, 0)
    return pl.pallas_call(
        functools.partial(_dsa_prompt_kernel, topk=topk),
        grid=(batch, nq),
        in_specs=[pl.BlockSpec((TQ, AB_WIDTH), row_tile),
                  pl.BlockSpec((TQ, AB_WIDTH), row_tile),
                  pl.BlockSpec((TQ, SMALL_W), row_tile),
                  pl.BlockSpec((seq, 2 * AB_WIDTH), whole),
                  pl.BlockSpec((seq, IDX_DIM), whole),
                  pl.BlockSpec((N_HEADS, 2 * TQ, TQ), lambda b, i: (0, 0, 0)),
                  pl.BlockSpec(far.shape, lambda b, i: (0, 0, 0))],
        out_specs=pl.BlockSpec((TQ, AB_WIDTH), row_tile),
        out_shape=jax.ShapeDtypeStruct((batch * seq, AB_WIDTH), F32),
        scratch_shapes=[pltpu.VMEM((nq, TQ, TQ), F32), pltpu.VMEM((TQ, AB_WIDTH), BF16),
                        pltpu.VMEM((N_HEADS, 1, TQ), F32), pltpu.VMEM((N_HEADS, 1, TQ), F32),
                        pltpu.VMEM((N_HEADS, AB_DIM, TQ), F32)],
        compiler_params=pltpu.CompilerParams(
            dimension_semantics=("parallel", "arbitrary"), vmem_limit_bytes=VMEM_LIMIT),
        name="dsa_prompt",
    )(qa, qi, small, kv_a, ki, band_t, far)


def _softplus_parts(z):
    t = jnp.log1p(jnp.exp(-jnp.abs(z)))
    return -(jnp.maximum(-z, 0.0) + t), -(jnp.maximum(z, 0.0) + t)


def _sb_prompt_kernel(q_ref, kv_ref, u_ref, o_ref, qb_ref, rs_ref, acc_ref):
    i = pl.program_id(1)
    row = lax.broadcasted_iota(jnp.int32, (TQ, TQ), 0)
    col = lax.broadcasted_iota(jnp.int32, (TQ, TQ), 1)
    qb_ref[...] = (q_ref[...] * (AB_DIM ** -0.5)).astype(BF16)
    rs_ref[...] = jnp.zeros(rs_ref.shape, F32)
    acc_ref[...] = jnp.zeros(acc_ref.shape, F32)

    def step(j, diag):
        rows = pl.ds(pl.multiple_of(j * TQ, TQ), TQ)
        for h in range(N_HEADS):
            lanes = slice(h * AB_DIM, (h + 1) * AB_DIM)
            z = _dot1(qb_ref[:, lanes], kv_ref[rows, lanes], NT)
            log_beta, log_1m = _softplus_parts(z)
            if diag:
                mask = col < row
                log_1m = jnp.where(mask, log_1m, 0.0)
            rs = rs_ref[h]
            suffix = _dot2(_split(log_1m), u_ref[...]) + rs
            w = jnp.exp(log_beta + suffix)
            if diag:
                w = jnp.where(mask, w, 0.0)
            vh = kv_ref[rows, AB_WIDTH + h * AB_DIM:AB_WIDTH + (h + 1) * AB_DIM]
            acc_ref[h] = acc_ref[h] + _dot1(w, vh)
            rs_ref[h] = rs + jnp.sum(log_1m, axis=-1, keepdims=True)

    step(i, True)
    lax.fori_loop(0, i, lambda t, c: (step(i - 1 - t, False), c)[1], 0)
    for h in range(N_HEADS):
        o_ref[:, h * AB_DIM:(h + 1) * AB_DIM] = acc_ref[h]


def _sb_prompt(qb, kv_b, batch, seq):
    nq = seq // TQ
    row_tile = lambda b, i: (b * nq + i, 0)
    return pl.pallas_call(
        _sb_prompt_kernel,
        grid=(batch, nq),
        in_specs=[pl.BlockSpec((TQ, AB_WIDTH), row_tile),
                  pl.BlockSpec((seq, 2 * AB_WIDTH), lambda b, i: (b, 0)),
                  pl.BlockSpec((TQ, TQ), lambda b, i: (0, 0))],
        out_specs=pl.BlockSpec((TQ, AB_WIDTH), row_tile),
        out_shape=jax.ShapeDtypeStruct((batch * seq, AB_WIDTH), F32),
        scratch_shapes=[pltpu.VMEM((TQ, AB_WIDTH), BF16), pltpu.VMEM((N_HEADS, TQ, 1), F32),
                        pltpu.VMEM((N_HEADS, TQ, AB_DIM), F32)],
        compiler_params=pltpu.CompilerParams(
            dimension_semantics=("parallel", "arbitrary"), vmem_limit_bytes=VMEM_LIMIT),
        name="sb_prompt",
    )(qb, kv_b, _strict_lower_ones(TQ))


def _moba_select(gate, n_valid, axis):
    nb = gate.shape[axis]
    blk = lax.broadcasted_iota(jnp.int32, gate.shape, axis)
    gate = jnp.where(blk < n_valid, gate, NEG_INF)
    sel = jnp.zeros(gate.shape, F32)
    for n in range(nb):
        gn = gate[n:n + 1, :] if axis == 0 else gate[:, n:n + 1]
        beats = (gate > gn) | ((gate == gn) & (blk < n))
        rank = jnp.sum(jnp.where(beats, 1.0, 0.0), axis=axis, keepdims=True)
        chosen = (rank < float(MOBA_TOPK)) & (blk == n) & (blk < n_valid)
        sel = jnp.where(chosen, 1.0, sel)
    return sel


def _moba_prompt_kernel(q_ref, k_ref, v_ref, km_ref, bandt_ref, far_ref, o_ref,
                        qb_ref, sel_ref, m_ref, l_ref, acc_ref):
    i = pl.program_id(2)
    hg = m_ref.shape[0]
    krow = lax.broadcasted_iota(jnp.int32, (TQ, TQ), 0)
    qcol = lax.broadcasted_iota(jnp.int32, (TQ, TQ), 1)
    blk = lax.broadcasted_iota(jnp.int32, sel_ref.shape[1:], 0)
    for h in range(hg):
        lanes = slice(h * C_DIM, (h + 1) * C_DIM)
        gate_t = _dot3(_split(km_ref[:, 0, lanes]), _split(q_ref[:, lanes]), NT)
        sel_ref[h] = _moba_select(gate_t, i, 0)
    qb_ref[...] = (q_ref[...] * (C_DIM ** -0.5)).astype(BF16)
    _flash_init(m_ref, l_ref, acc_ref)

    def step(j, kind):
        rows = pl.ds(pl.multiple_of(j * TQ, TQ), TQ)
        for h in range(hg):
            lanes = slice(h * C_DIM, (h + 1) * C_DIM)
            lg = _dot1(k_ref[rows, lanes], qb_ref[:, lanes], NT) + _tile_bias(bandt_ref, far_ref, h, kind)
            if kind == "diag":
                ok = krow <= qcol
            else:
                ok = jnp.sum(jnp.where(blk == j, sel_ref[h], 0.0), axis=0, keepdims=True) > 0.5
            lg = jnp.where(ok, lg, NEG_INF)
            _flash_update(h, lg, v_ref[rows, lanes], m_ref, l_ref, acc_ref)

    _for_key_tiles(i, step)
    _flash_finish(o_ref, l_ref, acc_ref)


def _moba_prompt(q, kv_c, kmean, band_t, far, batch, seq):
    nq = seq // TQ
    hg = MOBA_HEADS_PER_STEP
    gw = hg * C_DIM
    assert TQ == MOBA_BLOCK and N_HEADS % hg == 0
    q_tile = lambda b, g, i: (b * nq + i, g)
    return pl.pallas_call(
        _moba_prompt_kernel,
        grid=(batch, N_HEADS // hg, nq),
        in_specs=[pl.BlockSpec((TQ, gw), q_tile),
                  pl.BlockSpec((seq, gw), lambda b, g, i: (b, g)),
                  pl.BlockSpec((seq, gw), lambda b, g, i: (b, N_HEADS // hg + g)),
                  pl.BlockSpec((nq, 1, gw), lambda b, g, i: (b, 0, g)),
                  pl.BlockSpec((hg, 2 * TQ, TQ), lambda b, g, i: (g, 0, 0)),
                  pl.BlockSpec((hg, 1, LANES), lambda b, g, i: (g, 0, 0))],
        out_specs=pl.BlockSpec((TQ, gw), q_tile),
        out_shape=jax.ShapeDtypeStruct((batch * seq, C_WIDTH), F32),
        scratch_shapes=[pltpu.VMEM((TQ, gw), BF16), pltpu.VMEM((hg, nq, TQ), F32),
                        pltpu.VMEM((hg, 1, TQ), F32), pltpu.VMEM((hg, 1, TQ), F32),
                        pltpu.VMEM((hg, C_DIM, TQ), F32)],
        compiler_params=pltpu.CompilerParams(
            dimension_semantics=("parallel", "parallel", "arbitrary"), vmem_limit_bytes=VMEM_LIMIT),
        name="moba_prompt",
    )(q, kv_c, kv_c, kmean, band_t, far)


def _page_specs(n_pages, block, layer, tail=None):
    tail = (0,) * (len(block) - 2) if tail is None else tail
    return [pl.BlockSpec(block, lambda b, pt, p=p: (layer, pt[b, p]) + tail) for p in range(n_pages)]


def _pad_rows(x, rows):
    return jnp.concatenate([x, jnp.zeros((rows - x.shape[0], x.shape[1]), x.dtype)], axis=0)


def _head_rows(q, hm):
    return jnp.concatenate([q] * N_HEADS, axis=0) * hm


def _head_diag(res, hm, t):
    out = res[0:t, :] * hm[0:t, :]
    for h in range(1, N_HEADS):
        out = out + res[h * t:(h + 1) * t, :] * hm[h * t:(h + 1) * t, :]
    return out


def _sample_spec(t_new, width):
    return pl.BlockSpec((t_new, width), lambda b, pt: (b, 0))


def _new_page_t(rows):
    return _pad_rows(rows, PAGE).T


def _dsa_sample_kernel(pt_ref, qa_ref, qir_ref, sm_ref, kvn_ref, hm_ref, bias_ref, *rest,
                       n_pages, t_new, topk):
    kv_pages = rest[:n_pages]
    ki_pages = rest[n_pages:2 * n_pages]
    o_ref, sc_ref, lg_ref = rest[2 * n_pages:]
    wi = sm_ref[:, WI_LO:WI_HI]
    ki_new = _new_page_t(sm_ref[...])[:IDX_DIM, :]
    k_new = _new_page_t(kvn_ref[:, :AB_WIDTH])
    v_new = _new_page_t(kvn_ref[:, AB_WIDTH:])
    ki_t = lambda p: ki_pages[p][...] if p < n_pages else ki_new
    k_t = lambda p: kv_pages[p][0] if p < n_pages else k_new
    v_t = lambda p: kv_pages[p][1] if p < n_pages else v_new
    qir = _split(qir_ref[...])
    lane = lax.broadcasted_iota(jnp.int32, (t_new, PAGE), 1)
    trow = lax.broadcasted_iota(jnp.int32, (t_new, PAGE), 0)

    for p in range(n_pages + 1):
        r = jnp.maximum(_dot3(qir, _split(ki_t(p))) * (IDX_DIM ** -0.5), 0.0)
        sc = jnp.zeros((t_new, PAGE), F32)
        for hh in range(IDX_HEADS):
            sc = sc + wi[:, hh:hh + 1] * r[hh * t_new:(hh + 1) * t_new, :]
        if p == n_pages:
            sc = jnp.where(lane <= trow, sc, NEG_INF)
        sc_ref[p] = sc

    _topk_to_mask(sc_ref, n_pages + 1, topk, 1)

    hm = hm_ref[...]
    qrows = (_head_rows(qa_ref[...], hm) * (AB_DIM ** -0.5)).astype(BF16)
    m = jnp.full((N_HEADS * t_new, 1), M_INIT, F32)
    for p in range(n_pages + 1):
        mask = jnp.concatenate([sc_ref[p]] * N_HEADS, axis=0)
        lg = _dot1(qrows, k_t(p)) + bias_ref[p] + mask
        lg_ref[p] = lg
        m = jnp.maximum(m, jnp.max(lg, axis=-1, keepdims=True))
    l = jnp.zeros_like(m)
    acc = jnp.zeros((N_HEADS * t_new, AB_WIDTH), F32)
    for p in range(n_pages + 1):
        e = jnp.exp(lg_ref[p] - m)
        l = l + jnp.sum(e, axis=-1, keepdims=True)
        acc = acc + _dot1(e, v_t(p), NT)
    o_ref[...] = _head_diag(acc / l, hm, t_new)


def _dsa_sample(page_table, qa, qi_rows, small, kv_new, hm, bias, cache_kv, cache_ki, layer, t_new):
    batch, n_pages = page_table.shape
    topk = min(DSA_TOPK, (n_pages * PAGE + t_new) // 4)
    in_specs = [_sample_spec(t_new, AB_WIDTH),
                pl.BlockSpec((None, IDX_HEADS * t_new, IDX_DIM), lambda b, pt: (b, 0, 0)),
                _sample_spec(t_new, SMALL_W),
                _sample_spec(t_new, 2 * AB_WIDTH),
                pl.BlockSpec(hm.shape, lambda b, pt: (0, 0)),
                pl.BlockSpec(bias.shape, lambda b, pt: (0, 0, 0))]
    in_specs += _page_specs(n_pages, (None, None, 2, AB_WIDTH, PAGE), layer)
    in_specs += _page_specs(n_pages, (None, None, IDX_DIM, PAGE), layer)
    grid_spec = pltpu.PrefetchScalarGridSpec(
        num_scalar_prefetch=1, grid=(batch,), in_specs=in_specs,
        out_specs=_sample_spec(t_new, AB_WIDTH),
        scratch_shapes=[pltpu.VMEM((n_pages + 1, t_new, PAGE), F32),
                        pltpu.VMEM((n_pages + 1, N_HEADS * t_new, PAGE), F32)])
    return pl.pallas_call(
        functools.partial(_dsa_sample_kernel, n_pages=n_pages, t_new=t_new, topk=topk),
        grid_spec=grid_spec,
        out_shape=jax.ShapeDtypeStruct((batch * t_new, AB_WIDTH), F32),
        compiler_params=pltpu.CompilerParams(
            dimension_semantics=("arbitrary",), vmem_limit_bytes=VMEM_LIMIT),
        name="dsa_sample",
    )(page_table, qa, qi_rows, small, kv_new, hm, bias,
      *([cache_kv] * n_pages), *([cache_ki] * n_pages))


def _sb_sample_kernel(pt_ref, q_ref, kvn_ref, hm_ref, u_ref, *rest, n_pages, t_new):
    kv_pages = rest[:n_pages]
    o_ref = rest[n_pages]
    k_new = _new_page_t(kvn_ref[:, :AB_WIDTH])
    v_new = _new_page_t(kvn_ref[:, AB_WIDTH:])
    k_t = lambda p: kv_pages[p][0] if p < n_pages else k_new
    v_t = lambda p: kv_pages[p][1] if p < n_pages else v_new
    hm = hm_ref[...]
    u = u_ref[...]
    qrows = (_head_rows(q_ref[...], hm) * (AB_DIM ** -0.5)).astype(BF16)
    shape = (N_HEADS * t_new, PAGE)
    lane = lax.broadcasted_iota(jnp.int32, shape, 1)
    trow = lax.rem(lax.broadcasted_iota(jnp.int32, shape, 0), t_new)
    rs = jnp.zeros((N_HEADS * t_new, 1), F32)
    acc = jnp.zeros((N_HEADS * t_new, AB_WIDTH), F32)
    for p in reversed(range(n_pages + 1)):
        z = _dot1(qrows, k_t(p))
        log_beta, log_1m = _softplus_parts(z)
        if p == n_pages:
            mask = lane < trow
            log_1m = jnp.where(mask, log_1m, 0.0)
        suffix = _dot2(_split(log_1m), u) + rs
        w = jnp.exp(log_beta + suffix)
        if p == n_pages:
            w = jnp.where(mask, w, 0.0)
        acc = acc + _dot1(w, v_t(p), NT)
        rs = rs + jnp.sum(log_1m, axis=-1, keepdims=True)
    o_ref[...] = _head_diag(acc, hm, t_new)


def _sb_sample(page_table, qb, kv_new, hm, cache_kv, layer, t_new):
    batch, n_pages = page_table.shape
    in_specs = [_sample_spec(t_new, AB_WIDTH),
                _sample_spec(t_new, 2 * AB_WIDTH),
                pl.BlockSpec(hm.shape, lambda b, pt: (0, 0)),
                pl.BlockSpec((PAGE, PAGE), lambda b, pt: (0, 0))]
    in_specs += _page_specs(n_pages, (None, None, 2, AB_WIDTH, PAGE), layer)
    grid_spec = pltpu.PrefetchScalarGridSpec(
        num_scalar_prefetch=1, grid=(batch,), in_specs=in_specs,
        out_specs=_sample_spec(t_new, AB_WIDTH))
    return pl.pallas_call(
        functools.partial(_sb_sample_kernel, n_pages=n_pages, t_new=t_new),
        grid_spec=grid_spec,
        out_shape=jax.ShapeDtypeStruct((batch * t_new, AB_WIDTH), F32),
        compiler_params=pltpu.CompilerParams(
            dimension_semantics=("arbitrary",), vmem_limit_bytes=VMEM_LIMIT),
        name="sb_sample",
    )(page_table, qb, kv_new, hm, _strict_lower_ones(PAGE), *([cache_kv] * n_pages))


def _moba_sample_kernel(pt_ref, q_ref, kvn_ref, bias_ref, *rest, n_pages, t_new):
    kv_pages = rest[:n_pages]
    o_ref, km_ref, lg_ref = rest[n_pages:]
    pages_per_block = MOBA_BLOCK // PAGE
    n_blocks = n_pages // pages_per_block
    rows_per_tok = 2 * N_HEADS
    for n in range(n_blocks):
        tot = jnp.zeros((rows_per_tok, C_DIM), F32)
        for r in range(pages_per_block):
            page = kv_pages[n * pages_per_block + r][...]
            tot = tot + jnp.sum(page.reshape(PAGE, rows_per_tok, C_DIM), axis=0)
        km_ref[n * N_HEADS:(n + 1) * N_HEADS, :] = tot[:N_HEADS, :] / MOBA_BLOCK
    lane = lax.broadcasted_iota(jnp.int32, (t_new, PAGE), 1)
    trow = lax.broadcasted_iota(jnp.int32, (t_new, PAGE), 0)
    kvn = _pad_rows(kvn_ref[...], PAGE)
    for h in range(N_HEADS):
        q = q_ref[:, h * C_DIM:(h + 1) * C_DIM]
        km_h = km_ref[pl.ds(h, n_blocks, stride=N_HEADS), :]
        gate = _dot3(_split(q), _split(km_h), NT)
        sel = _moba_select(gate, n_blocks, 1)
        qs = (q * (C_DIM ** -0.5)).astype(BF16)
        m = jnp.full((t_new, 1), M_INIT, F32)
        for p in range(n_pages + 1):
            if p < n_pages:
                k = kv_pages[p][pl.ds(h, PAGE, stride=rows_per_tok), :]
                n = p // pages_per_block
                ok = sel[:, n:n + 1] > 0.5
            else:
                k = kvn[:, h * C_DIM:(h + 1) * C_DIM]
                ok = lane <= trow
            lg = jnp.where(ok, _dot1(qs, k, NT) + bias_ref[p, h * t_new:(h + 1) * t_new, :], NEG_INF)
            lg_ref[p] = lg
            m = jnp.maximum(m, jnp.max(lg, axis=-1, keepdims=True))
        l = jnp.zeros_like(m)
        acc = jnp.zeros((t_new, C_DIM), F32)
        for p in range(n_pages + 1):
            if p < n_pages:
                v = kv_pages[p][pl.ds(N_HEADS + h, PAGE, stride=rows_per_tok), :]
            else:
                v = kvn[:, C_WIDTH + h * C_DIM:C_WIDTH + (h + 1) * C_DIM]
            e = jnp.exp(lg_ref[p] - m)
            l = l + jnp.sum(e, axis=-1, keepdims=True)
            acc = acc + _dot1(e, v)
        o_ref[:, h * C_DIM:(h + 1) * C_DIM] = acc / l


def _moba_sample(page_table, q, kv_new, bias, cache_kv, layer, t_new):
    batch, n_pages = page_table.shape
    assert (n_pages * PAGE) % MOBA_BLOCK == 0 and t_new <= MOBA_BLOCK
    in_specs = [_sample_spec(t_new, C_WIDTH),
                _sample_spec(t_new, 2 * C_WIDTH),
                pl.BlockSpec(bias.shape, lambda b, pt: (0, 0, 0))]
    in_specs += _page_specs(n_pages, (None, None, PAGE * 2 * N_HEADS, C_DIM), layer)
    grid_spec = pltpu.PrefetchScalarGridSpec(
        num_scalar_prefetch=1, grid=(batch,), in_specs=in_specs,
        out_specs=_sample_spec(t_new, C_WIDTH),
        scratch_shapes=[pltpu.VMEM((n_pages * PAGE // MOBA_BLOCK * N_HEADS, C_DIM), F32),
                        pltpu.VMEM((n_pages + 1, t_new, PAGE), F32)])
    return pl.pallas_call(
        functools.partial(_moba_sample_kernel, n_pages=n_pages, t_new=t_new),
        grid_spec=grid_spec,
        out_shape=jax.ShapeDtypeStruct((batch * t_new, C_WIDTH), F32),
        compiler_params=pltpu.CompilerParams(
            dimension_semantics=("arbitrary",), vmem_limit_bytes=VMEM_LIMIT),
        name="moba_sample",
    )(page_table, q, kv_new, bias, *([cache_kv] * n_pages))


def kernel(x_prompt, x_sample, cache_a_kv, cache_a_kidx, cache_b_kv, cache_c_kv, page_table, ln_w,
           w_in_ab, w_out_ab, qn_a, kn_a, w_in_c, w_out_c, qn_c, kn_c, rel_bias):
    batch, seq, _ = x_prompt.shape
    dec_batch, t_new, _ = x_sample.shape
    n_pages = page_table.shape[1]
    past_len = n_pages * PAGE
    depth = ln_w.shape[0]
    assert cache_a_kv.shape[2] == PAGE and seq % TQ == 0

    xp = x_prompt.reshape(batch * seq, D_MODEL)
    xs = x_sample.reshape(dec_batch * t_new, D_MODEL)
    n_pool = cache_a_kv.shape[1]
    ca_kv = jnp.transpose(cache_a_kv, (0, 1, 3, 4, 5, 2)).reshape(-1, n_pool, 2, AB_WIDTH, PAGE)
    cb_kv = jnp.transpose(cache_b_kv, (0, 1, 3, 4, 5, 2)).reshape(-1, n_pool, 2, AB_WIDTH, PAGE)
    ca_ki = jnp.transpose(cache_a_kidx, (0, 1, 3, 2))
    cc_kv = cache_c_kv.reshape(-1, n_pool, PAGE * 2 * N_HEADS, C_DIM)

    band_t, far = _prompt_bias(rel_bias)
    bias_s = _sample_bias(rel_bias, past_len, t_new)
    hm_ab = _head_mask(AB_WIDTH, t_new)
    mnorm_ab = _block_mean_matrix(GW, AB_DIM)
    mnorm_c = _block_mean_matrix(GW, C_DIM)

    outs = {k: [] for k in ("a_kv_p", "a_kv_s", "a_ki_p", "a_ki_s", "b_kv_p", "b_kv_s", "c_kv_p", "c_kv_s")}
    for l in range(depth):
        i = l // 2
        if l % 2 == 0:
            w_main, w_small, norm_w = _ab_weights(w_in_ab[i], qn_a[i], kn_a[i])
            w_out = w_out_ab[i].astype(BF16)
            proj = lambda x: _proj_call(x, ln_w[l], w_main, norm_w, mnorm_ab, w_small,
                                        _AB_PLAN, _AB_WIDTHS, False)
            kv_a, kv_b, g, qa, qi, qb, ki, small = proj(xp)
            oa = _dsa_prompt(qa, qi, small, kv_a, ki, band_t, far, batch, seq)
            ob = _sb_prompt(qb, kv_b, batch, seq)
            xp = _out_call(xp, g, [oa, ob], w_out)
            outs["a_kv_p"].append(kv_a)
            outs["a_ki_p"].append(ki)
            outs["b_kv_p"].append(kv_b)

            kv_a, kv_b, g, qa, qi, qb, ki, small = proj(xs)
            qi_rows = qi.reshape(dec_batch, t_new, IDX_HEADS, IDX_DIM).transpose(0, 2, 1, 3)
            qi_rows = qi_rows.reshape(dec_batch, IDX_HEADS * t_new, IDX_DIM)
            oa = _dsa_sample(page_table, qa, qi_rows, small, kv_a, hm_ab, bias_s, ca_kv, ca_ki, i, t_new)
            ob = _sb_sample(page_table, qb, kv_b, hm_ab, cb_kv, i, t_new)
            xs = _out_call(xs, g, [oa, ob], w_out)
            outs["a_kv_s"].append(kv_a)
            outs["a_ki_s"].append(ki)
            outs["b_kv_s"].append(kv_b)
        else:
            w_main, norm_w = _c_weights(w_in_c[i], qn_c[i], kn_c[i])
            w_out = w_out_c[i].astype(BF16)
            proj = lambda x, km: _proj_call(x, ln_w[l], w_main, norm_w, mnorm_c, None,
                                            _C_PLAN, _C_WIDTHS, km)
            kv_c, g, q, kmean = proj(xp, True)
            o = _moba_prompt(q, kv_c, kmean, band_t, far, batch, seq)
            xp = _out_call(xp, g, [o], w_out)
            outs["c_kv_p"].append(kv_c)

            kv_c, g, q = proj(xs, False)
            o = _moba_sample(page_table, q, kv_c, bias_s, cc_kv, i, t_new)
            xs = _out_call(xs, g, [o], w_out)
            outs["c_kv_s"].append(kv_c)

    def kv(name, b, t, d):
        return jnp.stack(outs[name]).reshape(-1, b, t, 2, N_HEADS, d)

    return (xp.reshape(batch, seq, D_MODEL), xs.reshape(dec_batch, t_new, D_MODEL),
            kv("a_kv_p", batch, seq, AB_DIM), kv("a_kv_s", dec_batch, t_new, AB_DIM),
            jnp.stack(outs["a_ki_p"]).reshape(-1, batch, seq, IDX_DIM),
            jnp.stack(outs["a_ki_s"]).reshape(-1, dec_batch, t_new, IDX_DIM),
            kv("b_kv_p", batch, seq, AB_DIM), kv("b_kv_s", dec_batch, t_new, AB_DIM),
            kv("c_kv_p", batch, seq, C_DIM), kv("c_kv_s", dec_batch, t_new, C_DIM))
```

```python
import functools
import math

import numpy as np
import jax
import jax.numpy as jnp
from jax import lax
from jax.experimental import pallas as pl
from jax.experimental.pallas import tpu as pltpu

F32 = jnp.float32
BF16 = jnp.bfloat16
HI = lax.Precision.HIGHEST
NEG_INF = float("-inf")
M_INIT = -1e30

D_MODEL = 1024
N_HEADS = 8
AB_DIM = 64
AB_WIDTH = N_HEADS * AB_DIM
C_DIM = 128
C_WIDTH = N_HEADS * C_DIM
IDX_HEADS = 8
IDX_DIM = 64
DSA_TOPK = 256
MOBA_BLOCK = 256
MOBA_TOPK = 3
PAGE = 128
REL_BUCKETS = 32
REL_MAX_DIST = 128
EPS = 1e-6

LANES = 128
SUBLANES = 8
TQ = MOBA_BLOCK
GW = 512
TM = 512
MOBA_HEADS_PER_STEP = 4
SMALL_W = LANES
WI_LO, WI_HI = IDX_DIM, IDX_DIM + IDX_HEADS
NN = (((1,), (0,)), ((), ()))
NT = (((1,), (1,)), ((), ()))
TN = (((0,), (0,)), ((), ()))
VMEM_LIMIT = 48 * 1024 * 1024


def _split(x):
    hi = x.astype(BF16)
    return hi, (x - hi.astype(F32)).astype(BF16)


def _dg(a, b, dims):
    return lax.dot_general(a, b, dims, preferred_element_type=F32)


def _dot3(a, b, dims=NN):
    return _dg(a[0], b[0], dims) + _dg(a[0], b[1], dims) + _dg(a[1], b[0], dims)


def _dot2(a, b_exact, dims=NN):
    return _dg(a[0], b_exact, dims) + _dg(a[1], b_exact, dims)


def _dot1(a, b, dims=NN):
    return _dg(a.astype(BF16), b.astype(BF16), dims)


def _t5_bucket_np(dist):
    dist = np.maximum(np.asarray(dist, np.int64), 0)
    max_exact = REL_BUCKETS // 2
    d = np.maximum(dist, max_exact).astype(np.float64)
    val = np.log(d / max_exact) / math.log(REL_MAX_DIST / max_exact) * (REL_BUCKETS - max_exact)
    large = np.minimum(max_exact + val.astype(np.int64), REL_BUCKETS - 1)
    return np.where(dist < max_exact, dist, large).astype(np.int32)


def _bias_lookup(rel_bias, buckets):
    onehot = (jnp.asarray(buckets.reshape(-1))[:, None] == jnp.arange(REL_BUCKETS)[None, :]).astype(F32)
    out = jnp.dot(onehot, rel_bias.astype(F32), precision=HI)
    return out.T.reshape((N_HEADS,) + buckets.shape)


def _prompt_bias(rel_bias):
    c = np.arange(2 * TQ)[:, None]
    r = np.arange(TQ)[None, :]
    band = _bias_lookup(rel_bias, _t5_bucket_np(TQ + r - c))
    far_bucket = _t5_bucket_np(np.array([TQ + 1, 1 << 20]))
    assert far_bucket[0] == far_bucket[1]
    far = jnp.broadcast_to(rel_bias[int(far_bucket[0])].astype(F32)[:, None, None], (N_HEADS, 1, LANES))
    return band, far


def _sample_bias(rel_bias, past_len, t_new):
    n_pages = past_len // PAGE
    p = np.arange(n_pages + 1)[:, None, None]
    t = np.arange(t_new)[None, :, None]
    c = np.arange(PAGE)[None, None, :]
    b = _bias_lookup(rel_bias, _t5_bucket_np(past_len + t - (p * PAGE + c)))
    return jnp.transpose(b, (1, 0, 2, 3)).reshape(n_pages + 1, N_HEADS * t_new, PAGE)


def _head_mask(width, rows_per_head):
    d = width // N_HEADS
    h_row = np.arange(N_HEADS * rows_per_head)[:, None] // rows_per_head
    h_col = np.arange(width)[None, :] // d
    return jnp.asarray((h_row == h_col).astype(np.float32))


def _block_mean_matrix(width, d):
    g = np.arange(width) // d
    return jnp.asarray((g[:, None] == g[None, :]).astype(np.float32) / d).astype(BF16)


def _strict_lower_ones(n):
    idx = np.arange(n)
    return jnp.asarray((idx[:, None] > idx[None, :]).astype(np.float32)).astype(BF16)


def _proj_kernel(*refs, plan, n_main_out, has_small, has_kmean):
    it = iter(refs)
    x_ref, lnw_ref, w_ref, nw_ref, m_ref = (next(it) for _ in range(5))
    ws_ref = next(it) if has_small else None
    outs = [next(it) for _ in range(n_main_out)]
    ki_ref = next(it) if has_small else None
    small_ref = next(it) if has_small else None
    km_ref = next(it) if has_kmean else None
    hb_ref = next(it)
    j = pl.program_id(1)

    @pl.when(j == 0)
    def _():
        x = x_ref[...]
        h = x * lax.rsqrt(jnp.mean(x * x, axis=-1, keepdims=True) + EPS) * lnw_ref[...]
        hb = h.astype(BF16)
        hb_ref[...] = hb
        if has_small:
            s = _dg(hb, ws_ref[...], NN)
            lane = lax.broadcasted_iota(jnp.int32, s.shape, 1)
            is_ki = lane < IDX_DIM
            ms = jnp.sum(jnp.where(is_ki, s * s, 0.0), axis=-1, keepdims=True) / IDX_DIM
            sm = jnp.where(is_ki, s * lax.rsqrt(ms + EPS), s * (IDX_HEADS ** -0.5))
            small_ref[...] = sm
            ki_ref[...] = sm[:, :IDX_DIM]

    acc = _dg(hb_ref[...], w_ref[...], NN)
    for jj, (oi, off, nrow, kmean_here) in enumerate(plan):
        @pl.when(j == jj)
        def _(oi=oi, off=off, nrow=nrow, kmean_here=kmean_here):
            val = acc
            if nrow is not None:
                ms = _dot2(_split(acc * acc), m_ref[...])
                val = acc * lax.rsqrt(ms + EPS) * nw_ref[nrow:nrow + 1, :]
            outs[oi][:, off:off + GW] = val
            if kmean_here and has_kmean:
                for r in range(val.shape[0] // MOBA_BLOCK):
                    blk = val[r * MOBA_BLOCK:(r + 1) * MOBA_BLOCK, :]
                    km_ref[r, :, off:off + GW] = jnp.sum(blk, axis=0, keepdims=True) / MOBA_BLOCK


def _proj_call(x, ln_w, w_main, norm_w, mnorm, w_small, plan, out_widths, has_kmean):
    n = x.shape[0]
    tm = min(TM, n)
    assert n % tm == 0 and w_main.shape[1] == GW * len(plan)
    has_small = w_small is not None
    assert not has_kmean or tm % MOBA_BLOCK == 0
    row = lambda i, j: (i, 0)
    const = lambda i, j: (0, 0)
    in_specs = [
        pl.BlockSpec((tm, D_MODEL), row),
        pl.BlockSpec((1, D_MODEL), const),
        pl.BlockSpec((D_MODEL, GW), lambda i, j: (0, j)),
        pl.BlockSpec(norm_w.shape, const),
        pl.BlockSpec((GW, GW), const),
    ]
    args = [x, ln_w.reshape(1, D_MODEL), w_main, norm_w, mnorm]
    if has_small:
        in_specs.append(pl.BlockSpec((D_MODEL, SMALL_W), const))
        args.append(w_small)
    out_shape = [jax.ShapeDtypeStruct((n, w), F32) for w in out_widths]
    out_specs = [pl.BlockSpec((tm, w), row) for w in out_widths]
    if has_small:
        out_shape += [jax.ShapeDtypeStruct((n, IDX_DIM), F32), jax.ShapeDtypeStruct((n, SMALL_W), F32)]
        out_specs += [pl.BlockSpec((tm, IDX_DIM), row), pl.BlockSpec((tm, SMALL_W), row)]
    if has_kmean:
        out_shape.append(jax.ShapeDtypeStruct((n // MOBA_BLOCK, 1, C_WIDTH), F32))
        out_specs.append(pl.BlockSpec((tm // MOBA_BLOCK, 1, C_WIDTH), lambda i, j: (i, 0, 0)))
    kern = functools.partial(_proj_kernel, plan=tuple(plan), n_main_out=len(out_widths),
                             has_small=has_small, has_kmean=has_kmean)
    return pl.pallas_call(
        kern,
        grid=(n // tm, len(plan)),
        in_specs=in_specs,
        out_specs=out_specs,
        out_shape=out_shape,
        scratch_shapes=[pltpu.VMEM((tm, D_MODEL), BF16)],
        compiler_params=pltpu.CompilerParams(
            dimension_semantics=("parallel", "arbitrary"), vmem_limit_bytes=VMEM_LIMIT),
        name="proj_ab" if has_small else "proj_c",
    )(*args)


def _ab_weights(w_in, qn, kn):
    offs = np.cumsum([0, 512, 512, 512, 512, 512, 64, 8, 512, 512, 512, 512])
    qa, ka, va, ga, qi, ki, wi, qb, kb, vb, gb = (w_in[:, offs[k]:offs[k + 1]] for k in range(11))
    w_main = jnp.concatenate([ka, va, kb, vb, ga, gb, qa, qi, qb], axis=1)
    w_small = jnp.concatenate([ki, wi, jnp.zeros((D_MODEL, SMALL_W - IDX_DIM - IDX_HEADS), F32)], axis=1)
    norm_w = jnp.stack([jnp.tile(kn, N_HEADS), jnp.tile(qn, N_HEADS)])
    return w_main.astype(BF16), w_small.astype(BF16), norm_w


_AB_PLAN = [(0, 0, 0, False), (0, 512, None, False),
            (1, 0, None, False), (1, 512, None, False),
            (2, 0, None, False), (2, 512, None, False),
            (3, 0, 1, False),
            (4, 0, None, False),
            (5, 0, None, False)]
_AB_WIDTHS = [1024, 1024, 1024, 512, 512, 512]

_C_PLAN = [(0, 0, 0, True), (0, 512, 0, True),
           (0, 1024, None, False), (0, 1536, None, False),
           (1, 0, None, False), (1, 512, None, False),
           (2, 0, 1, False), (2, 512, 1, False)]
_C_WIDTHS = [2048, 1024, 1024]


def _c_weights(w_in, qn, kn):
    q, k, v, g = (w_in[:, c * C_WIDTH:(c + 1) * C_WIDTH] for c in range(4))
    w_main = jnp.concatenate([k, v, g, q], axis=1)
    norm_w = jnp.stack([jnp.tile(kn, GW // C_DIM), jnp.tile(qn, GW // C_DIM)])
    return w_main.astype(BF16), norm_w


def _out_kernel(*refs, n_o):
    x_ref, g_ref = refs[0], refs[1]
    o_refs = refs[2:2 + n_o]
    w_ref, y_ref = refs[2 + n_o:]
    g = g_ref[...]
    sg = g * (1.0 / (1.0 + jnp.exp(-g)))
    y = x_ref[...]
    off = 0
    for o_ref in o_refs:
        w = o_ref.shape[1]
        y = y + _dot1(o_ref[...] * sg[:, off:off + w], w_ref[off:off + w, :])
        off += w
    y_ref[...] = y


def _out_call(x, g, os_, w_out):
    n = x.shape[0]
    tm = min(TM, n)
    assert n % tm == 0
    row = lambda i: (i, 0)
    in_specs = [pl.BlockSpec((tm, D_MODEL), row), pl.BlockSpec((tm, D_MODEL), row)]
    in_specs += [pl.BlockSpec((tm, o.shape[1]), row) for o in os_]
    in_specs.append(pl.BlockSpec((D_MODEL, D_MODEL), lambda i: (0, 0)))
    return pl.pallas_call(
        functools.partial(_out_kernel, n_o=len(os_)),
        grid=(n // tm,),
        in_specs=in_specs,
        out_specs=pl.BlockSpec((tm, D_MODEL), row),
        out_shape=jax.ShapeDtypeStruct((n, D_MODEL), F32),
        compiler_params=pltpu.CompilerParams(
            dimension_semantics=("parallel",), vmem_limit_bytes=VMEM_LIMIT),
        name="out_proj",
    )(x, g, *os_, w_out)


def _tree(parts, comb):
    while len(parts) > 1:
        parts = [comb(parts[k], parts[k + 1]) for k in range(0, len(parts) - 1, 2)] + parts[len(parts) & ~1:]
    return parts[0]


def _fold_blocks(sc_ref, nblk, f, comb, init_val, key_axis):
    r, w = sc_ref.shape[1], sc_ref.shape[2]

    def vregs(x):
        if key_axis == 1:
            return _tree([x[:, c * LANES:(c + 1) * LANES] for c in range(w // LANES)], comb)
        return _tree([x[g * SUBLANES:(g + 1) * SUBLANES, :] for g in range(r // SUBLANES)], comb)

    init = jnp.full((r, LANES) if key_axis == 1 else (SUBLANES, w), init_val, F32)
    if isinstance(nblk, int):
        acc = init
        for j in range(nblk):
            acc = comb(acc, vregs(f(sc_ref[j], j)))
        return acc
    return lax.fori_loop(0, nblk, lambda j, acc: comb(acc, vregs(f(sc_ref[j], j))), init)


def _for_blocks(nblk, body):
    if isinstance(nblk, int):
        for j in range(nblk):
            body(j)
    else:
        lax.fori_loop(0, nblk, lambda j, c: (body(j), c)[1], 0)


def _topk_to_mask(sc_ref, nblk, k, key_axis):
    kf = float(k)
    keys_per_block = sc_ref.shape[1 + key_axis]
    red = -1 if key_axis == 1 else 0
    ones = lambda m: jnp.where(m, 1.0, 0.0)
    add, fmin, fmax = jnp.add, jnp.minimum, jnp.maximum
    fold = functools.partial(_fold_blocks, sc_ref, nblk, key_axis=key_axis)
    count = lambda pred: jnp.sum(fold(lambda s, j: ones(pred(s, j)), add, 0.0), axis=red, keepdims=True)

    nvalid = count(lambda s, j: s > NEG_INF)
    mn = jnp.min(fold(lambda s, j: jnp.where(s > NEG_INF, s, jnp.inf), fmin, jnp.inf), axis=red, keepdims=True)
    mx = jnp.max(fold(lambda s, j: s, fmax, NEG_INF), axis=red, keepdims=True)
    cmx = count(lambda s, j: s >= mx)
    cz_ge = count(lambda s, j: s >= 0.0)
    cz_gt = count(lambda s, j: s > 0.0)
    mpos = jnp.min(fold(lambda s, j: jnp.where(s > 0.0, s, jnp.inf), fmin, jnp.inf), axis=red, keepdims=True)
    few = nvalid <= kf
    top = cmx >= kf
    zero = (cz_gt < kf) & (cz_ge >= kf)
    v0 = jnp.where(few, NEG_INF, jnp.where(top, mx, 0.0))
    done0 = ones(few | top | zero)
    ca0 = jnp.where(cz_gt >= kf, cz_gt, nvalid)
    cb0 = jnp.where(cz_ge < kf, cz_ge, cmx)
    mn = jnp.where(cz_gt >= kf, mpos, mn)
    mx = jnp.where(cz_ge < kf, 0.0, mx)

    def cond(st):
        a, b, ca, cb, v, done = st
        return jnp.min(jnp.maximum(done, ones(ca - cb <= 2.0))) < 0.5

    def body(st):
        a, b, ca, cb, v, done = st
        p = a * 0.5 + b * 0.5
        stuck = (p <= a) | (p >= b)
        c = count(lambda s, j: s >= p)
        hit = c == kf
        is_done = done > 0.5
        v = jnp.where(is_done, v, jnp.where(stuck, a, jnp.where(hit, p, v)))
        frozen = is_done | stuck | hit
        low = c >= kf
        a, ca = jnp.where(frozen | ~low, a, p), jnp.where(frozen | ~low, ca, c)
        b, cb = jnp.where(frozen | low, b, p), jnp.where(frozen | low, cb, c)
        return a, b, ca, cb, v, ones(frozen)

    a, b, ca, cb, v, done = lax.while_loop(cond, body, (mn, mx, ca0, cb0, v0, done0))
    in_ab = lambda s: (s >= a) & (s < b)
    hi = jnp.max(fold(lambda s, j: jnp.where(in_ab(s), s, NEG_INF), fmax, NEG_INF), axis=red, keepdims=True)
    lo = jnp.min(fold(lambda s, j: jnp.where(in_ab(s), s, jnp.inf), fmin, jnp.inf), axis=red, keepdims=True)
    v = jnp.where(done > 0.5, v, jnp.where(kf - cb <= 1.0, hi, lo))

    need = kf - count(lambda s, j: s > v)
    neq = count(lambda s, j: s == v)
    excess = (neq > need) & (v > NEG_INF)
    has_ties = jnp.max(ones(excess)) > 0.5
    key_index = lambda s, j: j * keys_per_block + lax.broadcasted_iota(jnp.int32, s.shape, key_axis)

    @pl.when(has_ties)
    def _():
        nbits = (sc_ref.shape[0] * keys_per_block).bit_length()
        cut = jnp.zeros(v.shape, jnp.int32)
        for bit in reversed(range(nbits)):
            cnd = cut + (1 << bit)
            cn = count(lambda s, j: (s == v) & (key_index(s, j) < cnd))
            cut = jnp.where(cn <= need, cnd, cut)

        def write(j):
            s = sc_ref[j]
            sel = ((s > v) | ((s == v) & (key_index(s, j) < cut))) & (s > NEG_INF)
            sc_ref[j] = jnp.where(sel, 0.0, NEG_INF)

        _for_blocks(nblk, write)

    @pl.when(jnp.logical_not(has_ties))
    def _():
        def write(j):
            s = sc_ref[j]
            sc_ref[j] = jnp.where((s >= v) & (s > NEG_INF), 0.0, NEG_INF)

        _for_blocks(nblk, write)


def _flash_init(m_ref, l_ref, acc_ref):
    m_ref[...] = jnp.full(m_ref.shape, M_INIT, F32)
    l_ref[...] = jnp.zeros(l_ref.shape, F32)
    acc_ref[...] = jnp.zeros(acc_ref.shape, F32)


def _flash_update(h, lg, vh, m_ref, l_ref, acc_ref):
    m = m_ref[h]
    m_new = jnp.maximum(m, jnp.max(lg, axis=0, keepdims=True))
    alpha = jnp.exp(m - m_new)
    p = jnp.exp(lg - m_new)
    m_ref[h] = m_new
    l_ref[h] = alpha * l_ref[h] + jnp.sum(p, axis=0, keepdims=True)
    acc_ref[h] = alpha * acc_ref[h] + _dot1(vh, p, TN)


def _flash_finish(o_ref, l_ref, acc_ref):
    for h in range(acc_ref.shape[0]):
        acc_ref[h] = acc_ref[h] / l_ref[h]
    o_ref[...] = acc_ref[...].reshape(acc_ref.shape[0] * acc_ref.shape[1], acc_ref.shape[2]).T


def _for_key_tiles(i, step):
    lax.fori_loop(0, jnp.maximum(i - 1, 0), lambda j, c: (step(j, "far"), c)[1], 0)

    @pl.when(i >= 1)
    def _():
        step(i - 1, "prev")

    step(i, "diag")


def _tile_bias(bandt_ref, far_ref, h, kind):
    if kind == "far":
        return far_ref[h][:, :1]
    return bandt_ref[h, :TQ, :] if kind == "prev" else bandt_ref[h, TQ:, :]


def _dsa_prompt_kernel(qa_ref, qi_ref, sm_ref, kv_ref, ki_ref, bandt_ref, far_ref, o_ref,
                       sc_ref, qb_ref, m_ref, l_ref, acc_ref, *, topk):
    i = pl.program_id(1)
    krow = lax.broadcasted_iota(jnp.int32, (TQ, TQ), 0)
    qcol = lax.broadcasted_iota(jnp.int32, (TQ, TQ), 1)
    wi_t = sm_ref[...].T[WI_LO:WI_HI, :]
    qb_ref[...] = qi_ref[...].astype(BF16)

    def score_body(j, carry):
        kid = ki_ref[pl.ds(pl.multiple_of(j * TQ, TQ), TQ), :].astype(BF16)
        sc = jnp.zeros((TQ, TQ), F32)
        for hh in range(IDX_HEADS):
            s = _dg(kid, qb_ref[:, hh * IDX_DIM:(hh + 1) * IDX_DIM], NT) * (IDX_DIM ** -0.5)
            sc = sc + wi_t[hh:hh + 1, :] * jnp.maximum(s, 0.0)
        causal = (krow + j * TQ) <= (qcol + i * TQ)
        sc_ref[j] = jnp.where(causal, sc, NEG_INF)
        return carry

    lax.fori_loop(0, i + 1, score_body, 0)
    _topk_to_mask(sc_ref, i + 1, topk, 0)

    qb_ref[...] = (qa_ref[...] * (AB_DIM ** -0.5)).astype(BF16)
    _flash_init(m_ref, l_ref, acc_ref)

    def step(j, kind):
        rows = pl.ds(pl.multiple_of(j * TQ, TQ), TQ)
        for h in range(N_HEADS):
            lanes = slice(h * AB_DIM, (h + 1) * AB_DIM)
            lg = _dot1(kv_ref[rows, lanes], qb_ref[:, lanes], NT)
            lg = lg + _tile_bias(bandt_ref, far_ref, h, kind) + sc_ref[j]
            vh = kv_ref[rows, AB_WIDTH + h * AB_DIM:AB_WIDTH + (h + 1) * AB_DIM]
            _flash_update(h, lg, vh, m_ref, l_ref, acc_ref)

    _for_key_tiles(i, step)
    _flash_finish(o_ref, l_ref, acc_ref)


def _dsa_prompt(qa, qi, small, kv_a, ki, band_t, far, batch, seq):
    nq = seq // TQ
    topk = min(DSA_TOPK, seq // 4)
    row_tile = lambda b, i: (b * nq + i, 0)
    whole = lambda b, i: (b, 0)
    return pl.pallas_call(
        functools.partial(_dsa_prompt_kernel, topk=topk),
        grid=(batch, nq),
        in_specs=[pl.BlockSpec((TQ, AB_WIDTH), row_tile),
                  pl.BlockSpec((TQ, AB_WIDTH), row_tile),
                  pl.BlockSpec((TQ, SMALL_W), row_tile),
                  pl.BlockSpec((seq, 2 * AB_WIDTH), whole),
                  pl.BlockSpec((seq, IDX_DIM), whole),
                  pl.BlockSpec((N_HEADS, 2 * TQ, TQ), lambda b, i: (0, 0, 0)),
                  pl.BlockSpec(far.shape, lambda b, i: (0, 0, 0))],
        out_specs=pl.BlockSpec((TQ, AB_WIDTH), row_tile),
        out_shape=jax.ShapeDtypeStruct((batch * seq, AB_WIDTH), F32),
        scratch_shapes=[pltpu.VMEM((nq, TQ, TQ), F32), pltpu.VMEM((TQ, AB_WIDTH), BF16),
                        pltpu.VMEM((N_HEADS, 1, TQ), F32), pltpu.VMEM((N_HEADS, 1, TQ), F32),
                        pltpu.VMEM((N_HEADS, AB_DIM, TQ), F32)],
        compiler_params=pltpu.CompilerParams(
            dimension_semantics=("parallel", "arbitrary"), vmem_limit_bytes=VMEM_LIMIT),
        name="dsa_prompt",
    )(qa, qi, small, kv_a, ki, band_t, far)


def _softplus_parts(z):
    t = jnp.log(1.0 + jnp.exp(-jnp.abs(z)))
    return jnp.minimum(z, 0.0) - t, -jnp.maximum(z, 0.0) - t


def _sb_prompt_kernel(q_ref, kv_ref, u_ref, o_ref, qb_ref, rs_ref, acc_ref):
    i = pl.program_id(1)
    row = lax.broadcasted_iota(jnp.int32, (TQ, TQ), 0)
    col = lax.broadcasted_iota(jnp.int32, (TQ, TQ), 1)
    qb_ref[...] = (q_ref[...] * (AB_DIM ** -0.5)).astype(BF16)
    rs_ref[...] = jnp.zeros(rs_ref.shape, F32)
    acc_ref[...] = jnp.zeros(acc_ref.shape, F32)

    def step(j, diag):
        rows = pl.ds(pl.multiple_of(j * TQ, TQ), TQ)
        for h in range(N_HEADS):
            lanes = slice(h * AB_DIM, (h + 1) * AB_DIM)
            z = _dot1(qb_ref[:, lanes], kv_ref[rows, lanes], NT)
            log_beta, log_1m = _softplus_parts(z)
            if diag:
                mask = col < row
                log_1m = jnp.where(mask, log_1m, 0.0)
            rs = rs_ref[h]
            suffix = _dot2(_split(log_1m), u_ref[...]) + rs
            w = jnp.exp(log_beta + suffix)
            if diag:
                w = jnp.where(mask, w, 0.0)
            vh = kv_ref[rows, AB_WIDTH + h * AB_DIM:AB_WIDTH + (h + 1) * AB_DIM]
            acc_ref[h] = acc_ref[h] + _dot1(w, vh)
            rs_ref[h] = rs + jnp.sum(log_1m, axis=-1, keepdims=True)

    step(i, True)
    lax.fori_loop(0, i, lambda t, c: (step(i - 1 - t, False), c)[1], 0)
    for h in range(N_HEADS):
        o_ref[:, h * AB_DIM:(h + 1) * AB_DIM] = acc_ref[h]


def _sb_prompt(qb, kv_b, batch, seq):
    nq = seq // TQ
    row_tile = lambda b, i: (b * nq + i, 0)
    return pl.pallas_call(
        _sb_prompt_kernel,
        grid=(batch, nq),
        in_specs=[pl.BlockSpec((TQ, AB_WIDTH), row_tile),
                  pl.BlockSpec((seq, 2 * AB_WIDTH), lambda b, i: (b, 0)),
                  pl.BlockSpec((TQ, TQ), lambda b, i: (0, 0))],
        out_specs=pl.BlockSpec((TQ, AB_WIDTH), row_tile),
        out_shape=jax.ShapeDtypeStruct((batch * seq, AB_WIDTH), F32),
        scratch_shapes=[pltpu.VMEM((TQ, AB_WIDTH), BF16), pltpu.VMEM((N_HEADS, TQ, 1), F32),
                        pltpu.VMEM((N_HEADS, TQ, AB_DIM), F32)],
        compiler_params=pltpu.CompilerParams(
            dimension_semantics=("parallel", "arbitrary"), vmem_limit_bytes=VMEM_LIMIT),
        name="sb_prompt",
    )(qb, kv_b, _strict_lower_ones(TQ))


def _moba_select(gate, n_valid, axis):
    nb = gate.shape[axis]
    blk = lax.broadcasted_iota(jnp.int32, gate.shape, axis)
    gate = jnp.where(blk < n_valid, gate, NEG_INF)
    sel = jnp.zeros(gate.shape, F32)
    for n in range(nb):
        gn = gate[n:n + 1, :] if axis == 0 else gate[:, n:n + 1]
        beats = (gate > gn) | ((gate == gn) & (blk < n))
        rank = jnp.sum(jnp.where(beats, 1.0, 0.0), axis=axis, keepdims=True)
        chosen = (rank < float(MOBA_TOPK)) & (blk == n) & (blk < n_valid)
        sel = jnp.where(chosen, 1.0, sel)
    return sel


def _moba_prompt_kernel(q_ref, k_ref, v_ref, km_ref, bandt_ref, far_ref, o_ref,
                        qb_ref, sel_ref, m_ref, l_ref, acc_ref):
    i = pl.program_id(2)
    hg = m_ref.shape[0]
    krow = lax.broadcasted_iota(jnp.int32, (TQ, TQ), 0)
    qcol = lax.broadcasted_iota(jnp.int32, (TQ, TQ), 1)
    blk = lax.broadcasted_iota(jnp.int32, sel_ref.shape[1:], 0)
    for h in range(hg):
        lanes = slice(h * C_DIM, (h + 1) * C_DIM)
        gate_t = _dot3(_split(km_ref[:, 0, lanes]), _split(q_ref[:, lanes]), NT)
        sel_ref[h] = _moba_select(gate_t, i, 0)
    qb_ref[...] = (q_ref[...] * (C_DIM ** -0.5)).astype(BF16)
    _flash_init(m_ref, l_ref, acc_ref)

    def step(j, kind):
        rows = pl.ds(pl.multiple_of(j * TQ, TQ), TQ)
        for h in range(hg):
            lanes = slice(h * C_DIM, (h + 1) * C_DIM)
            lg = _dot1(k_ref[rows, lanes], qb_ref[:, lanes], NT) + _tile_bias(bandt_ref, far_ref, h, kind)
            if kind == "diag":
                ok = krow <= qcol
            else:
                ok = jnp.sum(jnp.where(blk == j, sel_ref[h], 0.0), axis=0, keepdims=True) > 0.5
            lg = jnp.where(ok, lg, NEG_INF)
            _flash_update(h, lg, v_ref[rows, lanes], m_ref, l_ref, acc_ref)

    _for_key_tiles(i, step)
    _flash_finish(o_ref, l_ref, acc_ref)


def _moba_prompt(q, kv_c, kmean, band_t, far, batch, seq):
    nq = seq // TQ
    hg = MOBA_HEADS_PER_STEP
    gw = hg * C_DIM
    assert TQ == MOBA_BLOCK and N_HEADS % hg == 0
    q_tile = lambda b, g, i: (b * nq + i, g)
    return pl.pallas_call(
        _moba_prompt_kernel,
        grid=(batch, N_HEADS // hg, nq),
        in_specs=[pl.BlockSpec((TQ, gw), q_tile),
                  pl.BlockSpec((seq, gw), lambda b, g, i: (b, g)),
                  pl.BlockSpec((seq, gw), lambda b, g, i: (b, N_HEADS // hg + g)),
                  pl.BlockSpec((nq, 1, gw), lambda b, g, i: (b, 0, g)),
                  pl.BlockSpec((hg, 2 * TQ, TQ), lambda b, g, i: (g, 0, 0)),
                  pl.BlockSpec((hg, 1, LANES), lambda b, g, i: (g, 0, 0))],
        out_specs=pl.BlockSpec((TQ, gw), q_tile),
        out_shape=jax.ShapeDtypeStruct((batch * seq, C_WIDTH), F32),
        scratch_shapes=[pltpu.VMEM((TQ, gw), BF16), pltpu.VMEM((hg, nq, TQ), F32),
                        pltpu.VMEM((hg, 1, TQ), F32), pltpu.VMEM((hg, 1, TQ), F32),
                        pltpu.VMEM((hg, C_DIM, TQ), F32)],
        compiler_params=pltpu.CompilerParams(
            dimension_semantics=("parallel", "parallel", "arbitrary"), vmem_limit_bytes=VMEM_LIMIT),
        name="moba_prompt",
    )(q, kv_c, kv_c, kmean, band_t, far)


def _page_specs(n_pages, block, layer, tail=None):
    tail = (0,) * (len(block) - 2) if tail is None else tail
    return [pl.BlockSpec(block, lambda b, pt, p=p: (layer, pt[b, p]) + tail) for p in range(n_pages)]


def _pad_rows(x, rows):
    return jnp.concatenate([x, jnp.zeros((rows - x.shape[0], x.shape[1]), x.dtype)], axis=0)


def _head_rows(q, hm):
    return jnp.concatenate([q] * N_HEADS, axis=0) * hm


def _head_diag(res, hm, t):
    out = res[0:t, :] * hm[0:t, :]
    for h in range(1, N_HEADS):
        out = out + res[h * t:(h + 1) * t, :] * hm[h * t:(h + 1) * t, :]
    return out


def _sample_spec(t_new, width):
    return pl.BlockSpec((t_new, width), lambda b, pt: (b, 0))


def _new_page_t(rows):
    return _pad_rows(rows, PAGE).T


def _dsa_sample_kernel(pt_ref, qa_ref, qir_ref, sm_ref, kvn_ref, hm_ref, bias_ref, *rest,
                       n_pages, t_new, topk):
    kv_pages = rest[:n_pages]
    ki_pages = rest[n_pages:2 * n_pages]
    o_ref, sc_ref, lg_ref = rest[2 * n_pages:]
    wi = sm_ref[:, WI_LO:WI_HI]
    ki_new = _new_page_t(sm_ref[...])[:IDX_DIM, :]
    k_new = _new_page_t(kvn_ref[:, :AB_WIDTH])
    v_new = _new_page_t(kvn_ref[:, AB_WIDTH:])
    ki_t = lambda p: ki_pages[p][...] if p < n_pages else ki_new
    k_t = lambda p: kv_pages[p][0] if p < n_pages else k_new
    v_t = lambda p: kv_pages[p][1] if p < n_pages else v_new
    qir = _split(qir_ref[...])
    lane = lax.broadcasted_iota(jnp.int32, (t_new, PAGE), 1)
    trow = lax.broadcasted_iota(jnp.int32, (t_new, PAGE), 0)

    for p in range(n_pages + 1):
        r = jnp.maximum(_dot3(qir, _split(ki_t(p))) * (IDX_DIM ** -0.5), 0.0)
        sc = jnp.zeros((t_new, PAGE), F32)
        for hh in range(IDX_HEADS):
            sc = sc + wi[:, hh:hh + 1] * r[hh * t_new:(hh + 1) * t_new, :]
        if p == n_pages:
            sc = jnp.where(lane <= trow, sc, NEG_INF)
        sc_ref[p] = sc

    _topk_to_mask(sc_ref, n_pages + 1, topk, 1)

    hm = hm_ref[...]
    qrows = (_head_rows(qa_ref[...], hm) * (AB_DIM ** -0.5)).astype(BF16)
    m = jnp.full((N_HEADS * t_new, 1), M_INIT, F32)
    for p in range(n_pages + 1):
        mask = jnp.concatenate([sc_ref[p]] * N_HEADS, axis=0)
        lg = _dot1(qrows, k_t(p)) + bias_ref[p] + mask
        lg_ref[p] = lg
        m = jnp.maximum(m, jnp.max(lg, axis=-1, keepdims=True))
    l = jnp.zeros_like(m)
    acc = jnp.zeros((N_HEADS * t_new, AB_WIDTH), F32)
    for p in range(n_pages + 1):
        e = jnp.exp(lg_ref[p] - m)
        l = l + jnp.sum(e, axis=-1, keepdims=True)
        acc = acc + _dot1(e, v_t(p), NT)
    o_ref[...] = _head_diag(acc / l, hm, t_new)


def _dsa_sample(page_table, qa, qi_rows, small, kv_new, hm, bias, cache_kv, cache_ki, layer, t_new):
    batch, n_pages = page_table.shape
    topk = min(DSA_TOPK, (n_pages * PAGE + t_new) // 4)
    in_specs = [_sample_spec(t_new, AB_WIDTH),
                pl.BlockSpec((None, IDX_HEADS * t_new, IDX_DIM), lambda b, pt: (b, 0, 0)),
                _sample_spec(t_new, SMALL_W),
                _sample_spec(t_new, 2 * AB_WIDTH),
                pl.BlockSpec(hm.shape, lambda b, pt: (0, 0)),
                pl.BlockSpec(bias.shape, lambda b, pt: (0, 0, 0))]
    in_specs += _page_specs(n_pages, (None, None, 2, AB_WIDTH, PAGE), layer)
    in_specs += _page_specs(n_pages, (None, None, IDX_DIM, PAGE), layer)
    grid_spec = pltpu.PrefetchScalarGridSpec(
        num_scalar_prefetch=1, grid=(batch,), in_specs=in_specs,
        out_specs=_sample_spec(t_new, AB_WIDTH),
        scratch_shapes=[pltpu.VMEM((n_pages + 1, t_new, PAGE), F32),
                        pltpu.VMEM((n_pages + 1, N_HEADS * t_new, PAGE), F32)])
    return pl.pallas_call(
        functools.partial(_dsa_sample_kernel, n_pages=n_pages, t_new=t_new, topk=topk),
        grid_spec=grid_spec,
        out_shape=jax.ShapeDtypeStruct((batch * t_new, AB_WIDTH), F32),
        compiler_params=pltpu.CompilerParams(
            dimension_semantics=("arbitrary",), vmem_limit_bytes=VMEM_LIMIT),
        name="dsa_sample",
    )(page_table, qa, qi_rows, small, kv_new, hm, bias,
      *([cache_kv] * n_pages), *([cache_ki] * n_pages))


def _sb_sample_kernel(pt_ref, q_ref, kvn_ref, hm_ref, u_ref, *rest, n_pages, t_new):
    kv_pages = rest[:n_pages]
    o_ref = rest[n_pages]
    k_new = _new_page_t(kvn_ref[:, :AB_WIDTH])
    v_new = _new_page_t(kvn_ref[:, AB_WIDTH:])
    k_t = lambda p: kv_pages[p][0] if p < n_pages else k_new
    v_t = lambda p: kv_pages[p][1] if p < n_pages else v_new
    hm = hm_ref[...]
    u = u_ref[...]
    qrows = (_head_rows(q_ref[...], hm) * (AB_DIM ** -0.5)).astype(BF16)
    shape = (N_HEADS * t_new, PAGE)
    lane = lax.broadcasted_iota(jnp.int32, shape, 1)
    trow = lax.rem(lax.broadcasted_iota(jnp.int32, shape, 0), t_new)
    rs = jnp.zeros((N_HEADS * t_new, 1), F32)
    acc = jnp.zeros((N_HEADS * t_new, AB_WIDTH), F32)
    for p in reversed(range(n_pages + 1)):
        z = _dot1(qrows, k_t(p))
        log_beta, log_1m = _softplus_parts(z)
        if p == n_pages:
            mask = lane < trow
            log_1m = jnp.where(mask, log_1m, 0.0)
        suffix = _dot2(_split(log_1m), u) + rs
        w = jnp.exp(log_beta + suffix)
        if p == n_pages:
            w = jnp.where(mask, w, 0.0)
        acc = acc + _dot1(w, v_t(p), NT)
        rs = rs + jnp.sum(log_1m, axis=-1, keepdims=True)
    o_ref[...] = _head_diag(acc, hm, t_new)


def _sb_sample(page_table, qb, kv_new, hm, cache_kv, layer, t_new):
    batch, n_pages = page_table.shape
    in_specs = [_sample_spec(t_new, AB_WIDTH),
                _sample_spec(t_new, 2 * AB_WIDTH),
                pl.BlockSpec(hm.shape, lambda b, pt: (0, 0)),
                pl.BlockSpec((PAGE, PAGE), lambda b, pt: (0, 0))]
    in_specs += _page_specs(n_pages, (None, None, 2, AB_WIDTH, PAGE), layer)
    grid_spec = pltpu.PrefetchScalarGridSpec(
        num_scalar_prefetch=1, grid=(batch,), in_specs=in_specs,
        out_specs=_sample_spec(t_new, AB_WIDTH))
    return pl.pallas_call(
        functools.partial(_sb_sample_kernel, n_pages=n_pages, t_new=t_new),
        grid_spec=grid_spec,
        out_shape=jax.ShapeDtypeStruct((batch * t_new, AB_WIDTH), F32),
        compiler_params=pltpu.CompilerParams(
            dimension_semantics=("arbitrary",), vmem_limit_bytes=VMEM_LIMIT),
        name="sb_sample",
    )(page_table, qb, kv_new, hm, _strict_lower_ones(PAGE), *([cache_kv] * n_pages))


def _moba_sample_kernel(pt_ref, q_ref, kvn_ref, bias_ref, *rest, n_pages, t_new):
    kv_pages = rest[:n_pages]
    o_ref, km_ref, lg_ref = rest[n_pages:]
    pages_per_block = MOBA_BLOCK // PAGE
    n_blocks = n_pages // pages_per_block
    rows_per_tok = 2 * N_HEADS
    for n in range(n_blocks):
        tot = jnp.zeros((rows_per_tok, C_DIM), F32)
        for r in range(pages_per_block):
            page = kv_pages[n * pages_per_block + r][...]
            tot = tot + jnp.sum(page.reshape(PAGE, rows_per_tok, C_DIM), axis=0)
        km_ref[n * N_HEADS:(n + 1) * N_HEADS, :] = tot[:N_HEADS, :] / MOBA_BLOCK
    lane = lax.broadcasted_iota(jnp.int32, (t_new, PAGE), 1)
    trow = lax.broadcasted_iota(jnp.int32, (t_new, PAGE), 0)
    kvn = _pad_rows(kvn_ref[...], PAGE)
    for h in range(N_HEADS):
        q = q_ref[:, h * C_DIM:(h + 1) * C_DIM]
        km_h = km_ref[pl.ds(h, n_blocks, stride=N_HEADS), :]
        gate = _dot3(_split(q), _split(km_h), NT)
        sel = _moba_select(gate, n_blocks, 1)
        qs = (q * (C_DIM ** -0.5)).astype(BF16)
        m = jnp.full((t_new, 1), M_INIT, F32)
        for p in range(n_pages + 1):
            if p < n_pages:
                k = kv_pages[p][pl.ds(h, PAGE, stride=rows_per_tok), :]
                n = p // pages_per_block
                ok = sel[:, n:n + 1] > 0.5
            else:
                k = kvn[:, h * C_DIM:(h + 1) * C_DIM]
                ok = lane <= trow
            lg = jnp.where(ok, _dot1(qs, k, NT) + bias_ref[p, h * t_new:(h + 1) * t_new, :], NEG_INF)
            lg_ref[p] = lg
            m = jnp.maximum(m, jnp.max(lg, axis=-1, keepdims=True))
        l = jnp.zeros_like(m)
        acc = jnp.zeros((t_new, C_DIM), F32)
        for p in range(n_pages + 1):
            if p < n_pages:
                v = kv_pages[p][pl.ds(N_HEADS + h, PAGE, stride=rows_per_tok), :]
            else:
                v = kvn[:, C_WIDTH + h * C_DIM:C_WIDTH + (h + 1) * C_DIM]
            e = jnp.exp(lg_ref[p] - m)
            l = l + jnp.sum(e, axis=-1, keepdims=True)
            acc = acc + _dot1(e, v)
        o_ref[:, h * C_DIM:(h + 1) * C_DIM] = acc / l


def _moba_sample(page_table, q, kv_new, bias, cache_kv, layer, t_new):
    batch, n_pages = page_table.shape
    assert (n_pages * PAGE) % MOBA_BLOCK == 0 and t_new <= MOBA_BLOCK
    in_specs = [_sample_spec(t_new, C_WIDTH),
                _sample_spec(t_new, 2 * C_WIDTH),
                pl.BlockSpec(bias.shape, lambda b, pt: (0, 0, 0))]
    in_specs += _page_specs(n_pages, (None, None, PAGE * 2 * N_HEADS, C_DIM), layer)
    grid_spec = pltpu.PrefetchScalarGridSpec(
        num_scalar_prefetch=1, grid=(batch,), in_specs=in_specs,
        out_specs=_sample_spec(t_new, C_WIDTH),
        scratch_shapes=[pltpu.VMEM((n_pages * PAGE // MOBA_BLOCK * N_HEADS, C_DIM), F32),
                        pltpu.VMEM((n_pages + 1, t_new, PAGE), F32)])
    return pl.pallas_call(
        functools.partial(_moba_sample_kernel, n_pages=n_pages, t_new=t_new),
        grid_spec=grid_spec,
        out_shape=jax.ShapeDtypeStruct((batch * t_new, C_WIDTH), F32),
        compiler_params=pltpu.CompilerParams(
            dimension_semantics=("arbitrary",), vmem_limit_bytes=VMEM_LIMIT),
        name="moba_sample",
    )(page_table, q, kv_new, bias, *([cache_kv] * n_pages))


def kernel(x_prompt, x_sample, cache_a_kv, cache_a_kidx, cache_b_kv, cache_c_kv, page_table, ln_w,
           w_in_ab, w_out_ab, qn_a, kn_a, w_in_c, w_out_c, qn_c, kn_c, rel_bias):
    batch, seq, _ = x_prompt.shape
    dec_batch, t_new, _ = x_sample.shape
    n_pages = page_table.shape[1]
    past_len = n_pages * PAGE
    depth = ln_w.shape[0]
    assert cache_a_kv.shape[2] == PAGE and seq % TQ == 0

    xp = x_prompt.reshape(batch * seq, D_MODEL)
    xs = x_sample.reshape(dec_batch * t_new, D_MODEL)
    n_pool = cache_a_kv.shape[1]
    ca_kv = jnp.transpose(cache_a_kv, (0, 1, 3, 4, 5, 2)).reshape(-1, n_pool, 2, AB_WIDTH, PAGE)
    cb_kv = jnp.transpose(cache_b_kv, (0, 1, 3, 4, 5, 2)).reshape(-1, n_pool, 2, AB_WIDTH, PAGE)
    ca_ki = jnp.transpose(cache_a_kidx, (0, 1, 3, 2))
    cc_kv = cache_c_kv.reshape(-1, n_pool, PAGE * 2 * N_HEADS, C_DIM)

    band_t, far = _prompt_bias(rel_bias)
    bias_s = _sample_bias(rel_bias, past_len, t_new)
    hm_ab = _head_mask(AB_WIDTH, t_new)
    mnorm_ab = _block_mean_matrix(GW, AB_DIM)
    mnorm_c = _block_mean_matrix(GW, C_DIM)

    outs = {k: [] for k in ("a_kv_p", "a_kv_s", "a_ki_p", "a_ki_s", "b_kv_p", "b_kv_s", "c_kv_p", "c_kv_s")}
    for l in range(depth):
        i = l // 2
        if l % 2 == 0:
            w_main, w_small, norm_w = _ab_weights(w_in_ab[i], qn_a[i], kn_a[i])
            w_out = w_out_ab[i].astype(BF16)
            proj = lambda x: _proj_call(x, ln_w[l], w_main, norm_w, mnorm_ab, w_small,
                                        _AB_PLAN, _AB_WIDTHS, False)
            kv_a, kv_b, g, qa, qi, qb, ki, small = proj(xp)
            oa = _dsa_prompt(qa, qi, small, kv_a, ki, band_t, far, batch, seq)
            ob = _sb_prompt(qb, kv_b, batch, seq)
            xp = _out_call(xp, g, [oa, ob], w_out)
            outs["a_kv_p"].append(kv_a)
            outs["a_ki_p"].append(ki)
            outs["b_kv_p"].append(kv_b)

            kv_a, kv_b, g, qa, qi, qb, ki, small = proj(xs)
            qi_rows = qi.reshape(dec_batch, t_new, IDX_HEADS, IDX_DIM).transpose(0, 2, 1, 3)
            qi_rows = qi_rows.reshape(dec_batch, IDX_HEADS * t_new, IDX_DIM)
            oa = _dsa_sample(page_table, qa, qi_rows, small, kv_a, hm_ab, bias_s, ca_kv, ca_ki, i, t_new)
            ob = _sb_sample(page_table, qb, kv_b, hm_ab, cb_kv, i, t_new)
            xs = _out_call(xs, g, [oa, ob], w_out)
            outs["a_kv_s"].append(kv_a)
            outs["a_ki_s"].append(ki)
            outs["b_kv_s"].append(kv_b)
        else:
            w_main, norm_w = _c_weights(w_in_c[i], qn_c[i], kn_c[i])
            w_out = w_out_c[i].astype(BF16)
            proj = lambda x, km: _proj_call(x, ln_w[l], w_main, norm_w, mnorm_c, None,
                                            _C_PLAN, _C_WIDTHS, km)
            kv_c, g, q, kmean = proj(xp, True)
            o = _moba_prompt(q, kv_c, kmean, band_t, far, batch, seq)
            xp = _out_call(xp, g, [o], w_out)
            outs["c_kv_p"].append(kv_c)

            kv_c, g, q = proj(xs, False)
            o = _moba_sample(page_table, q, kv_c, bias_s, cc_kv, i, t_new)
            xs = _out_call(xs, g, [o], w_out)
            outs["c_kv_s"].append(kv_c)

    def kv(name, b, t, d):
        return jnp.stack(outs[name]).reshape(-1, b, t, 2, N_HEADS, d)

    return (xp.reshape(batch, seq, D_MODEL), xs.reshape(dec_batch, t_new, D_MODEL),
            kv("a_kv_p", batch, seq, AB_DIM), kv("a_kv_s", dec_batch, t_new, AB_DIM),
            jnp.stack(outs["a_ki_p"]).reshape(-1, batch, seq, IDX_DIM),
            jnp.stack(outs["a_ki_s"]).reshape(-1, dec_batch, t_new, IDX_DIM),
            kv("b_kv_p", batch, seq, AB_DIM), kv("b_kv_s", dec_batch, t_new, AB_DIM),
            kv("c_kv_p", batch, seq, C_DIM), kv("c_kv_s", dec_batch, t_new, C_DIM))
```
